```python
import math
import jax
import jax.numpy as jnp
from jax import lax
import numpy as np

D_MODEL = 1024
BATCH = 8
SEQ = 2048
DEPTH = 4
DEC_BATCH = 32
DEC_SEQ = 2048
PAST_LEN = 128

N_MIXERS = 4
D_FF = 2816
RMS_EPS = 1e-6
CONV_K = 5

SSD_D_INNER = 2 * D_MODEL
SSD_HEAD_DIM = 64
SSD_N_HEADS = SSD_D_INNER // SSD_HEAD_DIM
SSD_N_GROUPS = 8
SSD_D_STATE = 128
SSD_CHUNK = 128
SSD_CONV_DIM = SSD_D_INNER + 2 * SSD_N_GROUPS * SSD_D_STATE
SSD_IN_DIM = SSD_D_INNER + SSD_CONV_DIM + 2 * SSD_N_HEADS

GDN_N_K_HEADS = 8
GDN_N_V_HEADS = 16
GDN_HEAD_K = 128
GDN_HEAD_V = 128
GDN_CHUNK = 64
GDN_QK_DIM = GDN_N_K_HEADS * GDN_HEAD_K
GDN_V_DIM = GDN_N_V_HEADS * GDN_HEAD_V
GDN_CONV_DIM = 2 * GDN_QK_DIM + GDN_V_DIM
GDN_IN_DIM = GDN_CONV_DIM + GDN_V_DIM + 3 * GDN_N_V_HEADS

ATT_N_HEADS = 16
ATT_N_KV_HEADS = 4
ATT_HEAD_DIM = D_MODEL // ATT_N_HEADS
ATT_WINDOW = 128
ATT_QKV_DIM = (ATT_N_HEADS + 2 * ATT_N_KV_HEADS) * ATT_HEAD_DIM
ROPE_THETA = 10000.0

RWKV_HEAD = 64
RWKV_N_HEADS = D_MODEL // RWKV_HEAD
RWKV_DECAY_LORA = 64
RWKV_AAA_LORA = 64
RWKV_GATE_LORA = 128
RWKV_LN_EPS = 64e-5

kernel_name = 'hybrid_bidir_ssd_gdn_swa_rwkv7_macaron'


def rmsnorm(x, g, eps=RMS_EPS):
    xf = x.astype(jnp.float32)
    y = xf * lax.rsqrt(jnp.mean(xf * xf, axis=-1, keepdims=True) + eps)
    return (y * g.astype(jnp.float32)).astype(x.dtype)


def l2norm(x, eps=1e-6):
    xf = x.astype(jnp.float32)
    return (xf * lax.rsqrt(jnp.sum(xf * xf, axis=-1, keepdims=True) + eps)).astype(x.dtype)


def swiglu(x, w_gu, w_down):
    gate, up = jnp.split(x @ w_gu, 2, axis=-1)
    return (jax.nn.silu(gate) * up) @ w_down


def flip_seq(t):
    return jnp.flip(t, axis=1)


def centred_dwconv(x, w, b):
    pad = CONV_K // 2
    y = lax.conv_general_dilated(x, w[:, None, :], window_strides=(1,), padding=[(pad, pad)],
                                 dimension_numbers=('NWC', 'WIO', 'NWC'),
                                 feature_group_count=x.shape[-1])
    return y + b


def rope(x, pos):
    half = x.shape[-1] // 2
    inv_freq = ROPE_THETA ** (-jnp.arange(half, dtype=jnp.float32) / half)
    ang = pos.astype(jnp.float32)[:, None] * inv_freq[None, :]
    cos = jnp.cos(ang)[None, :, None, :]
    sin = jnp.sin(ang)[None, :, None, :]
    xf = x.astype(jnp.float32)
    x1, x2 = xf[..., :half], xf[..., half:]
    return jnp.concatenate([x1 * cos - x2 * sin, x2 * cos + x1 * sin], axis=-1).astype(x.dtype)


def ssd_scan(x, dt, a, bm, cm):
    b, l, g, r, p = x.shape
    n = bm.shape[-1]
    q = SSD_CHUNK
    c = l // q
    x = x.reshape(b, c, q, g, r, p)
    dt = dt.reshape(b, c, q, g, r)
    bm = bm.reshape(b, c, q, g, n)
    cm = cm.reshape(b, c, q, g, n)
    a_cs = jnp.cumsum(dt * a, axis=2)
    xdt = x * dt[..., None].astype(x.dtype)
    causal = jnp.tril(jnp.ones((q, q), dtype=bool))
    seg = a_cs[:, :, :, None] - a_cs[:, :, None, :]
    decay = jnp.exp(jnp.where(causal[:, :, None, None], seg, -jnp.inf)).astype(x.dtype)
    cb = jnp.einsum('bclgn,bcsgn->bclsg', cm, bm)
    y_diag = jnp.einsum('bclsgr,bcsgrp->bclgrp', cb[..., None] * decay, xdt)
    decay_to_end = jnp.exp(a_cs[:, :, -1:] - a_cs).astype(x.dtype)
    chunk_states = jnp.einsum('bclgn,bclgr,bclgrp->bcgrpn', bm, decay_to_end, xdt)
    chunk_decay = jnp.exp(a_cs[:, :, -1]).astype(x.dtype)

    def step(h, inp):
        s_c, d_c = inp
        return h * d_c[..., None, None] + s_c, h

    h0 = jnp.zeros((b, g, r, p, n), x.dtype)
    _, h_prev = lax.scan(step, h0, (jnp.moveaxis(chunk_states, 1, 0), jnp.moveaxis(chunk_decay, 1, 0)))
    h_prev = jnp.moveaxis(h_prev, 0, 1)
    y_off = jnp.einsum('bclgn,bcgrpn,bclgr->bclgrp', cm, h_prev, jnp.exp(a_cs).astype(x.dtype))
    return (y_diag + y_off).reshape(b, l, g, r, p)


def ssd_mixer(u, w_in, conv_w, conv_b, a_log, dt_bias, d_skip, norm_w, w_out):
    b, l, _ = u.shape
    g, r = SSD_N_GROUPS, SSD_N_HEADS // SSD_N_GROUPS
    gn = SSD_N_GROUPS * SSD_D_STATE
    z, xbc, dt_raw = jnp.split(u @ w_in, [SSD_D_INNER, SSD_D_INNER + SSD_CONV_DIM], axis=-1)
    xbc = jax.nn.silu(centred_dwconv(xbc, conv_w, conv_b))
    xs, bm, cm = jnp.split(xbc, [SSD_D_INNER, SSD_D_INNER + gn], axis=-1)
    xs = xs.reshape(b, l, g, r, SSD_HEAD_DIM)
    bm = bm.reshape(b, l, g, SSD_D_STATE)
    cm = cm.reshape(b, l, g, SSD_D_STATE)
    dt = jax.nn.softplus(dt_raw.astype(jnp.float32).reshape(b, l, 2, g, r)
                         + dt_bias.astype(jnp.float32).reshape(2, g, r))
    a = -jnp.exp(a_log.astype(jnp.float32)).reshape(2, g, r)
    y_fwd = ssd_scan(xs, dt[:, :, 0], a[0], bm, cm)
    y_bwd = flip_seq(ssd_scan(flip_seq(xs), flip_seq(dt[:, :, 1]), a[1], flip_seq(bm), flip_seq(cm)))
    y = (y_fwd + y_bwd + xs * d_skip.reshape(g, r, 1)).reshape(b, l, SSD_D_INNER)
    return rmsnorm(y * jax.nn.silu(z), norm_w) @ w_out


def gdn_chunk_scan(q, k, v, g, beta):
    b, l, h, dk = k.shape
    dv = v.shape[-1]
    cs = GDN_CHUNK
    n = l // cs
    dtype = v.dtype

    def to_chunks(t):
        return jnp.moveaxis(t.reshape(b, n, cs, h, *t.shape[3:]), 3, 2)

    q = to_chunks(q * (dk ** -0.5))
    k = to_chunks(k)
    v = to_chunks(v)
    beta = to_chunks(beta)
    g = jnp.cumsum(to_chunks(g), axis=-1)
    tril = jnp.tril(jnp.ones((cs, cs), dtype=bool))
    strict = jnp.tril(jnp.ones((cs, cs), dtype=bool), -1)
    decay = jnp.exp(jnp.where(tril, g[..., :, None] - g[..., None, :], -jnp.inf)).astype(dtype)
    kk = jnp.einsum('bnhid,bnhjd->bnhij', k, k)
    lower = jnp.where(strict, beta[..., :, None] * kk * decay, 0.0).astype(jnp.float32)
    t_mat = jnp.eye(cs, dtype=jnp.float32) + lower
    rhs = jnp.concatenate([v * beta[..., None],
                           k * (beta * jnp.exp(g).astype(dtype))[..., None]], axis=-1).astype(jnp.float32)
    sol = lax.linalg.triangular_solve(t_mat, rhs, left_side=True, lower=True).astype(dtype)
    u_vals, w_keys = sol[..., :dv], sol[..., dv:]
    qk = jnp.where(tril, jnp.einsum('bnhid,bnhjd->bnhij', q, k) * decay, 0.0).astype(dtype)
    q_dec = q * jnp.exp(g)[..., None].astype(dtype)
    k_dec = k * jnp.exp(g[..., -1:] - g)[..., None].astype(dtype)
    g_end = jnp.exp(g[..., -1]).astype(dtype)

    def step(s, inp):
        qk_c, qd_c, kd_c, u_c, w_c, ge_c = inp
        v_new = u_c - jnp.einsum('bhcd,bhde->bhce', w_c, s)
        o = jnp.einsum('bhcd,bhde->bhce', qd_c, s) + jnp.einsum('bhij,bhje->bhie', qk_c, v_new)
        s = s * ge_c[..., None, None] + jnp.einsum('bhcd,bhce->bhde', kd_c, v_new)
        return s, o

    s0 = jnp.zeros((b, h, dk, dv), dtype)
    xs = tuple(jnp.moveaxis(t, 1, 0) for t in (qk, q_dec, k_dec, u_vals, w_keys, g_end))
    _, o = lax.scan(step, s0, xs)
    return jnp.transpose(o, (1, 0, 3, 2, 4)).reshape(b, l, h, dv)


def gdn_mixer(u, w_in, conv_w, conv_b, a_log, dt_bias, norm_w, w_out):
    b, l, _ = u.shape
    hk, hv = GDN_N_K_HEADS, GDN_N_V_HEADS
    qkv, z, beta_raw, a_raw = jnp.split(
        u @ w_in, [GDN_CONV_DIM, GDN_CONV_DIM + GDN_V_DIM, GDN_CONV_DIM + GDN_V_DIM + hv], axis=-1)
    qkv = jax.nn.silu(centred_dwconv(qkv, conv_w, conv_b))
    q, k, v = jnp.split(qkv, [GDN_QK_DIM, 2 * GDN_QK_DIM], axis=-1)
    rep = hv // hk
    q = jnp.repeat(l2norm(q.reshape(b, l, hk, GDN_HEAD_K)), rep, axis=2)
    k = jnp.repeat(l2norm(k.reshape(b, l, hk, GDN_HEAD_K)), rep, axis=2)
    v = v.reshape(b, l, hv, GDN_HEAD_V)
    beta = jax.nn.sigmoid(beta_raw)
    gdec = -jnp.exp(a_log.astype(jnp.float32)) * jax.nn.softplus(
        a_raw.astype(jnp.float32).reshape(b, l, 2, hv) + dt_bias.astype(jnp.float32))
    o_fwd = gdn_chunk_scan(q, k, v, gdec[:, :, 0], beta)
    o_bwd = flip_seq(gdn_chunk_scan(flip_seq(q), flip_seq(k), flip_seq(v),
                                    flip_seq(gdec[:, :, 1]), flip_seq(beta)))
    o = rmsnorm(o_fwd + o_bwd, norm_w) * jax.nn.silu(z.reshape(b, l, hv, GDN_HEAD_V))
    return o.reshape(b, l, GDN_V_DIM) @ w_out


def window_attention(u, w_qkv, sinks, w_out):
    b, l, _ = u.shape
    h, kvh, dh, win = ATT_N_HEADS, ATT_N_KV_HEADS, ATT_HEAD_DIM, ATT_WINDOW
    rep = h // kvh
    nb = l // win
    q, k, v = jnp.split(u @ w_qkv, [h * dh, (h + kvh) * dh], axis=-1)
    pos = jnp.arange(l)
    q = rope(q.reshape(b, l, h, dh), pos).reshape(b, nb, win, kvh, rep, dh)
    k = rope(k.reshape(b, l, kvh, dh), pos)
    v = v.reshape(b, l, kvh, dh)

    def band(t):
        tp = jnp.pad(t, ((0, 0), (win, win), (0, 0), (0, 0)))
        return jnp.concatenate([tp[:, o * win:o * win + l].reshape(b, nb, win, kvh, dh) for o in range(3)],
                               axis=2)

    kb, vb = band(k), band(v)
    qpos = jnp.arange(nb)[:, None] * win + jnp.arange(win)[None, :]
    kpos = (jnp.arange(nb)[:, None] - 1) * win + jnp.arange(3 * win)[None, :]
    valid = ((jnp.abs(kpos[:, None, :] - qpos[:, :, None]) <= win)
             & (kpos[:, None, :] >= 0) & (kpos[:, None, :] < l))
    s = jnp.einsum('bnqgrd,bnkgd->bngrqk', q, kb).astype(jnp.float32) * (dh ** -0.5)
    s = jnp.where(valid[None, :, None, None], s, -jnp.inf)
    sink = sinks.astype(jnp.float32).reshape(kvh, rep)[None, None, :, :, None, None]
    m = jnp.maximum(jnp.max(s, axis=-1, keepdims=True), sink)
    e = jnp.exp(s - m)
    probs = (e / (jnp.sum(e, axis=-1, keepdims=True) + jnp.exp(sink - m))).astype(v.dtype)
    o = jnp.einsum('bngrqk,bnkgd->bnqgrd', probs, vb).reshape(b, l, h * dh)
    return o @ w_out


def rwkv7_scan(r, w, k, v, kk, a, reverse):
    b, l, nh, n = r.shape

    def step(s, inp):
        r_t, w_t, k_t, v_t, kk_t, a_t = inp
        sa = jnp.einsum('bhvk,bhk->bhv', s, kk_t)
        s = (s * w_t[:, :, None, :] - sa[..., None] * (kk_t * a_t)[:, :, None, :]
             + v_t[..., None] * k_t[:, :, None, :])
        return s, jnp.einsum('bhvk,bhk->bhv', s, r_t)

    xs = tuple(jnp.moveaxis(t, 1, 0) for t in (r, w, k, v, kk, a))
    s0 = jnp.zeros((b, nh, n, n), r.dtype)
    _, o = lax.scan(step, s0, xs, reverse=reverse)
    return jnp.moveaxis(o, 0, 1)


def rwkv7_mixer(u, x_mu, w_rkv, w0, w1, w2, a0, a1, a2, g1, g2, k_k, k_a, r_k, lnx_w, lnx_b, w_out):
    b, l, d = u.shape
    nh, n = RWKV_N_HEADS, RWKV_HEAD
    up = jnp.pad(u, ((0, 0), (1, 1), (0, 0)))
    xx = 0.5 * (up[:, :-2] + up[:, 2:]) - u
    xm = u[None] + xx[None] * x_mu[:, None, None, :]
    r, k, v = jnp.einsum('sbld,sde->sble', xm[:3], w_rkv)
    xw, xa, xg = xm[3], xm[4], xm[5]
    wl = jnp.tanh(jnp.einsum('bld,sdr->sblr', xw, w1))
    wlog = -jax.nn.softplus(-(w0[:, None, None, :] + jnp.einsum('sblr,srd->sbld', wl, w2)).astype(jnp.float32)) - 0.5
    decay = jnp.exp(-jnp.exp(wlog)).astype(u.dtype)
    a = jax.nn.sigmoid(a0 + (xa @ a1) @ a2)
    gate = jax.nn.sigmoid(xg @ g1) @ g2

    def heads(t):
        return t.reshape(*t.shape[:-1], nh, n)

    kk = l2norm(heads(k * k_k))
    k = k * (1.0 + (a - 1.0) * k_a)
    r_h, k_h, v_h, a_h, dec_h = heads(r), heads(k), heads(v), heads(a), heads(decay)
    o = (rwkv7_scan(r_h, dec_h[0], k_h, v_h, kk, a_h, reverse=False)
         + rwkv7_scan(r_h, dec_h[1], k_h, v_h, kk, a_h, reverse=True))
    of = o.astype(jnp.float32)
    mu = jnp.mean(of, axis=-1, keepdims=True)
    var = jnp.mean(jnp.square(of - mu), axis=-1, keepdims=True)
    o = ((of - mu) * lax.rsqrt(var + RWKV_LN_EPS)).astype(u.dtype).reshape(b, l, d) * lnx_w + lnx_b
    bonus = jnp.sum(r_h * k_h * heads(r_k), axis=-1, keepdims=True) * v_h
    return ((o + bonus.reshape(b, l, d)) * gate) @ w_out


def _n_layers_of(m):
    return (DEPTH - m + N_MIXERS - 1) // N_MIXERS


def setup_inputs(seed: int = 0) -> dict:
    key = jax.random.key(seed)
    keys = iter(jax.random.split(key, 64))
    f32 = jnp.float32

    def normal(shape, scale):
        return jax.random.normal(next(keys), shape, f32) * scale

    def gain(shape):
        return 1.0 + normal(shape, 0.02)

    def uniform(shape, lo, hi):
        return jax.random.uniform(next(keys), shape, f32, lo, hi)

    def dt_bias_init(shape):
        dt = jnp.exp(uniform(shape, math.log(1e-3), math.log(1e-1)))
        return dt + jnp.log(-jnp.expm1(-dt))

    na, nb, nc, nd = (_n_layers_of(m) for m in range(N_MIXERS))
    d, f = D_MODEL, D_FF
    return {
        'x_prompt': normal((BATCH, SEQ, d), 1.0),
        'x_sample': normal((DEC_BATCH, DEC_SEQ, d), 1.0),
        'ffn1_norm': gain((DEPTH, d)),
        'ffn1_w_gu': normal((DEPTH, d, 2 * f), d ** -0.5),
        'ffn1_w_down': normal((DEPTH, f, d), f ** -0.5),
        'mix_norm': gain((DEPTH, d)),
        'ffn2_norm': gain((DEPTH, d)),
        'ffn2_w_gu': normal((DEPTH, d, 2 * f), d ** -0.5),
        'ffn2_w_down': normal((DEPTH, f, d), f ** -0.5),
        'ssd_w_in': normal((na, d, SSD_IN_DIM), d ** -0.5),
        'ssd_conv_w': normal((na, CONV_K, SSD_CONV_DIM), CONV_K ** -0.5),
        'ssd_conv_b': normal((na, SSD_CONV_DIM), 0.02),
        'ssd_a_log': jnp.log(uniform((na, 2, SSD_N_HEADS), 1.0, 16.0)),
        'ssd_dt_bias': dt_bias_init((na, 2, SSD_N_HEADS)),
        'ssd_d': gain((na, SSD_N_HEADS)),
        'ssd_norm': gain((na, SSD_D_INNER)),
        'ssd_w_out': normal((na, SSD_D_INNER, d), SSD_D_INNER ** -0.5),
        'gdn_w_in': normal((nb, d, GDN_IN_DIM), d ** -0.5),
        'gdn_conv_w': normal((nb, CONV_K, GDN_CONV_DIM), CONV_K ** -0.5),
        'gdn_conv_b': normal((nb, GDN_CONV_DIM), 0.02),
        'gdn_a_log': jnp.log(uniform((nb, 2, GDN_N_V_HEADS), 1.0, 16.0)),
        'gdn_dt_bias': dt_bias_init((nb, 2, GDN_N_V_HEADS)),
        'gdn_norm': gain((nb, GDN_HEAD_V)),
        'gdn_w_out': normal((nb, GDN_V_DIM, d), GDN_V_DIM ** -0.5),
        'att_w_qkv': normal((nc, d, ATT_QKV_DIM), d ** -0.5),
        'att_sinks': normal((nc, ATT_N_HEADS), 0.5),
        'att_w_out': normal((nc, ATT_N_HEADS * ATT_HEAD_DIM, d), (ATT_N_HEADS * ATT_HEAD_DIM) ** -0.5),
        'rwkv_x_mu': uniform((nd, 6, d), 0.0, 1.0),
        'rwkv_w_rkv': normal((nd, 3, d, d), d ** -0.5),
        'rwkv_w0': uniform((nd, 2, d), -6.0, 1.0),
        'rwkv_w1': normal((nd, 2, d, RWKV_DECAY_LORA), d ** -0.5),
        'rwkv_w2': normal((nd, 2, RWKV_DECAY_LORA, d), 0.1 * RWKV_DECAY_LORA ** -0.5),
        'rwkv_a0': normal((nd, d), 0.1),
        'rwkv_a1': normal((nd, d, RWKV_AAA_LORA), d ** -0.5),
        'rwkv_a2': normal((nd, RWKV_AAA_LORA, d), 0.1 * RWKV_AAA_LORA ** -0.5),
        'rwkv_g1': normal((nd, d, RWKV_GATE_LORA), d ** -0.5),
        'rwkv_g2': normal((nd, RWKV_GATE_LORA, d), RWKV_GATE_LORA ** -0.5),
        'rwkv_k_k': 0.85 + normal((nd, d), 0.02),
        'rwkv_k_a': gain((nd, d)),
        'rwkv_r_k': normal((nd, d), 0.1),
        'rwkv_lnx_w': gain((nd, d)),
        'rwkv_lnx_b': normal((nd, d), 0.02),
        'rwkv_w_out': normal((nd, d, d), d ** -0.5),
        'final_norm': gain((d,)),
    }


def reference(x_prompt, x_sample,
              ffn1_norm, ffn1_w_gu, ffn1_w_down, mix_norm, ffn2_norm, ffn2_w_gu, ffn2_w_down,
              ssd_w_in, ssd_conv_w, ssd_conv_b, ssd_a_log, ssd_dt_bias, ssd_d, ssd_norm, ssd_w_out,
              gdn_w_in, gdn_conv_w, gdn_conv_b, gdn_a_log, gdn_dt_bias, gdn_norm, gdn_w_out,
              att_w_qkv, att_sinks, att_w_out,
              rwkv_x_mu, rwkv_w_rkv, rwkv_w0, rwkv_w1, rwkv_w2, rwkv_a0, rwkv_a1, rwkv_a2,
              rwkv_g1, rwkv_g2, rwkv_k_k, rwkv_k_a, rwkv_r_k, rwkv_lnx_w, rwkv_lnx_b, rwkv_w_out,
              final_norm):
    def token_mixer(h, i):
        m, j = i % N_MIXERS, i // N_MIXERS
        if m == 0:
            return ssd_mixer(h, ssd_w_in[j], ssd_conv_w[j], ssd_conv_b[j], ssd_a_log[j], ssd_dt_bias[j],
                             ssd_d[j], ssd_norm[j], ssd_w_out[j])
        if m == 1:
            return gdn_mixer(h, gdn_w_in[j], gdn_conv_w[j], gdn_conv_b[j], gdn_a_log[j], gdn_dt_bias[j],
                             gdn_norm[j], gdn_w_out[j])
        if m == 2:
            return window_attention(h, att_w_qkv[j], att_sinks[j], att_w_out[j])
        return rwkv7_mixer(h, rwkv_x_mu[j], rwkv_w_rkv[j], rwkv_w0[j], rwkv_w1[j], rwkv_w2[j],
                           rwkv_a0[j], rwkv_a1[j], rwkv_a2[j], rwkv_g1[j], rwkv_g2[j], rwkv_k_k[j],
                           rwkv_k_a[j], rwkv_r_k[j], rwkv_lnx_w[j], rwkv_lnx_b[j], rwkv_w_out[j])

    def trunk(x):
        for i in range(DEPTH):
            x = x + 0.5 * swiglu(rmsnorm(x, ffn1_norm[i]), ffn1_w_gu[i], ffn1_w_down[i])
            x = x + token_mixer(rmsnorm(x, mix_norm[i]), i)
            x = x + 0.5 * swiglu(rmsnorm(x, ffn2_norm[i]), ffn2_w_gu[i], ffn2_w_down[i])
        return rmsnorm(x, final_norm)

    y_prompt = trunk(x_prompt)
    y_sample = trunk(x_sample)
    return (y_prompt, y_sample)
```

```python
import functools
import math

import jax
import jax.numpy as jnp
from jax import lax
from jax.experimental import pallas as pl
from jax.experimental.pallas import tpu as pltpu

F32 = jnp.float32
BF16 = jnp.bfloat16

V7X_VMEM_LIMIT_BYTES = 56 * 1024 * 1024
LANES = 128

RMS_EPS = 1e-6
CONV_K = 5
N_MIXERS = 4

SSD_HEAD_DIM = 64
SSD_N_GROUPS = 8
SSD_HEADS_PER_GROUP = 4
SSD_D_STATE = 128
SSD_CHUNK = 128
SSD_D_INNER = 2048

GDN_N_K_HEADS = 8
GDN_N_V_HEADS = 16
GDN_HEAD = 128
GDN_CHUNK = 64
GDN_QK_DIM = 1024
GDN_V_DIM = 2048

ATT_N_HEADS = 16
ATT_N_KV_HEADS = 4
ATT_HEAD_DIM = 64
ATT_WINDOW = 128
ROPE_THETA = 10000.0

RWKV_HEAD = 64
RWKV_CHUNK = 64
RWKV_LN_EPS = 64e-5

PROJ_PAD = 6400

TOKEN_TILE = 512
FFN_CHUNK = 256


def _cparams(*sem):
    return pltpu.CompilerParams(dimension_semantics=sem, vmem_limit_bytes=V7X_VMEM_LIMIT_BYTES)


def _resident(shape):
    nd = len(shape)
    return pl.BlockSpec(shape, lambda *_: (0,) * nd, pipeline_mode=pl.Buffered(1))


def _sigmoid(x):
    return 1.0 / (1.0 + jnp.exp(-x))


def _silu(x):
    return x * _sigmoid(x)


def _softplus(x):
    return jnp.maximum(x, 0.0) + jnp.log1p(jnp.exp(-jnp.abs(x)))


def _rms(x, g):
    return x * lax.rsqrt(jnp.mean(x * x, axis=-1, keepdims=True) + RMS_EPS) * g


def _dot(a, b):
    return jnp.dot(a.astype(BF16), b.astype(BF16), preferred_element_type=F32)


def _dot_nt(a, b):
    return lax.dot_general(a.astype(BF16), b.astype(BF16), (((1,), (1,)), ((), ())), preferred_element_type=F32)


def _dot_tn(a, b):
    return lax.dot_general(a.astype(BF16), b.astype(BF16), (((0,), (0,)), ((), ())), preferred_element_type=F32)


def _dot_f32(a, b):
    return jnp.dot(a, b, precision=lax.Precision.HIGHEST, preferred_element_type=F32)


def _tri_apply(n, x, size):
    steps = int(math.log2(size))
    m = n
    for s in range(steps):
        x = x + _dot_f32(m, x)
        if s + 1 < steps:
            m = _dot_f32(m, m)
    return x


def _dwconv_silu(x, w, b):
    n = x.shape[0]
    row = lax.broadcasted_iota(jnp.int32, x.shape, 0)
    half = CONV_K // 2
    acc = x * w[half:half + 1, :] + b
    for k in range(-half, half + 1):
        if k == 0:
            continue
        xr = pltpu.roll(x, (-k) % n, axis=0)
        ok = (row + k >= 0) & (row + k < n)
        acc = acc + jnp.where(ok, xr, 0.0) * w[k + half:k + half + 1, :]
    return _silu(acc)


def _seg_sum2(x):
    lane = lax.broadcasted_iota(jnp.int32, x.shape, 1)
    lo = lane < 64
    s_lo = jnp.sum(jnp.where(lo, x, 0.0), axis=1, keepdims=True)
    s_hi = jnp.sum(jnp.where(lo, 0.0, x), axis=1, keepdims=True)
    return jnp.where(lo, s_lo, s_hi)


def _tri_masks(n, forward):
    li = lax.broadcasted_iota(jnp.int32, (n, n), 0)
    si = lax.broadcasted_iota(jnp.int32, (n, n), 1)
    if forward:
        return li >= si, li > si
    return li <= si, li < si


def _ffn_body(x_ref, g_ref, wg_ref, wu_ref, wd_ref, fg_ref, o_ref, xn_ref, acc_ref, *, n_chunks, final):
    x = x_ref[...]
    xn_ref[...] = _rms(x, g_ref[...]).astype(BF16)
    acc_ref[...] = jnp.zeros_like(acc_ref)

    def chunk(c, carry):
        xb = xn_ref[...]
        gate = jnp.dot(xb, wg_ref[c], preferred_element_type=F32)
        up = jnp.dot(xb, wu_ref[c], preferred_element_type=F32)
        h = (_silu(gate) * up).astype(BF16)
        acc_ref[...] += jnp.dot(h, wd_ref[c], preferred_element_type=F32)
        return carry

    lax.fori_loop(0, n_chunks, chunk, 0)
    y = x + 0.5 * acc_ref[...]
    if final:
        y = _rms(y, fg_ref[...])
    o_ref[...] = y


def _ffn(x, g, wg3, wu3, wd3, final_g, final):
    t, d = x.shape
    tm = TOKEN_TILE
    body = functools.partial(_ffn_body, n_chunks=wg3.shape[0], final=final)
    return pl.pallas_call(
        body,
        grid=(t // tm,),
        in_specs=[pl.BlockSpec((tm, d), lambda i: (i, 0)),
                  _resident((1, d)), _resident(wg3.shape), _resident(wu3.shape), _resident(wd3.shape),
                  _resident((1, d))],
        out_specs=pl.BlockSpec((tm, d), lambda i: (i, 0)),
        out_shape=jax.ShapeDtypeStruct((t, d), F32),
        scratch_shapes=[pltpu.VMEM((tm, d), BF16), pltpu.VMEM((tm, d), F32)],
        compiler_params=_cparams("parallel"),
        name="ffn",
    )(x, g, wg3, wu3, wd3, final_g)


def _proj_body(x_ref, g_ref, w_ref, o_ref, xn_ref):
    @pl.when(pl.program_id(1) == 0)
    def _():
        xn_ref[...] = _rms(x_ref[...], g_ref[...]).astype(BF16)

    o_ref[...] = jnp.dot(xn_ref[...], w_ref[...], preferred_element_type=F32)


def _proj(x, g, w, tn):
    t, d = x.shape
    n = w.shape[1]
    tm = TOKEN_TILE
    return pl.pallas_call(
        _proj_body,
        grid=(t // tm, n // tn),
        in_specs=[pl.BlockSpec((tm, d), lambda i, j: (i, 0)),
                  _resident((1, d)),
                  pl.BlockSpec((d, tn), lambda i, j: (0, j))],
        out_specs=pl.BlockSpec((tm, tn), lambda i, j: (i, j)),
        out_shape=jax.ShapeDtypeStruct((t, n), F32),
        scratch_shapes=[pltpu.VMEM((tm, d), BF16)],
        compiler_params=_cparams("parallel", "arbitrary"),
        name="norm_proj",
    )(x, g, w)


def _ssd_prep_body(p_ref, bias_ref, alog_ref, dt_ref, cs_ref, *, q):
    raw = p_ref[:, 0:LANES]
    dt = _softplus(raw + bias_ref[...])
    dta = dt * (-jnp.exp(alog_ref[...]))
    dt_ref[...] = dt
    incl_f, _ = _tri_masks(q, True)
    incl_b, _ = _tri_masks(q, False)
    tril = incl_f.astype(F32)
    triu = incl_b.astype(F32)
    lane = lax.broadcasted_iota(jnp.int32, (q, LANES), 1)
    n_heads = SSD_N_GROUPS * SSD_HEADS_PER_GROUP
    for c in range(raw.shape[0] // q):
        blk = dta[c * q:(c + 1) * q, :]
        cs_ref[c * q:(c + 1) * q, :] = jnp.where(lane < n_heads, _dot_f32(tril, blk), _dot_f32(triu, blk))


def _ssd_prep(p, bias, alog):
    t = p.shape[0]
    tm = TOKEN_TILE
    blk = pl.BlockSpec((tm, LANES), lambda i: (i, 0))
    return pl.pallas_call(
        functools.partial(_ssd_prep_body, q=SSD_CHUNK),
        grid=(t // tm,),
        in_specs=[pl.BlockSpec((tm, 2 * LANES), lambda i: (i, (PROJ_PAD - 2 * LANES) // (2 * LANES))),
                  _resident((1, LANES)), _resident((1, LANES))],
        out_specs=[blk, blk],
        out_shape=[jax.ShapeDtypeStruct((t, LANES), F32)] * 2,
        compiler_params=_cparams("parallel"),
        name="ssd_prep",
    )(p, bias, alog)


def _ssd_core_body(xs_ref, bm_ref, cm_ref, wx_ref, wb_ref, wc_ref, bx_ref, bb_ref, bc_ref, dsk_ref,
                   dtr_ref, csr_ref, csc_ref, o_ref, xs_s, b_s, c_s, y_s, st_ref, *, q, n_chunks):
    hd, nh = SSD_HEAD_DIM, SSD_HEADS_PER_GROUP
    xs_s[...] = _dwconv_silu(xs_ref[...], wx_ref[...], bx_ref[...])
    b_s[...] = _dwconv_silu(bm_ref[...], wb_ref[...], bb_ref[...])
    c_s[...] = _dwconv_silu(cm_ref[...], wc_ref[...], bc_ref[...]).astype(BF16)
    st_ref[...] = jnp.zeros_like(st_ref)

    def chunk(c, forward):
        r0 = pl.multiple_of(c * q, q)
        incl, _ = _tri_masks(q, forward)
        cc = c_s[pl.ds(r0, q), :]
        bc = b_s[pl.ds(r0, q), :]
        cb = _dot_nt(cc, bc)
        bct = bc.T
        xc = xs_s[pl.ds(r0, q), :].astype(BF16)
        acols = csc_ref[pl.ds(r0, q), :]
        arows = csr_ref[c]
        dtrows = dtr_ref[c]
        outs = []
        for h in range(nh):
            i = h if forward else nh + h
            acol = acols[:, i:i + 1]
            arow = arows[i:i + 1, :]
            dtr = dtrows[i:i + 1, :]
            xh = xc[:, h * hd:(h + 1) * hd]
            lmat = jnp.where(incl, jnp.exp(acol - arow), 0.0) * cb * dtr
            tot = arow[:, q - 1:q] if forward else arow[:, 0:1]
            st = st_ref[i]
            y = _dot(lmat, xh) + _dot(cc, st) * jnp.exp(acol)
            st_ref[i] = st * jnp.exp(tot) + _dot(bct * (jnp.exp(tot - arow) * dtr), xh)
            outs.append(y)
        y = jnp.concatenate(outs, axis=1)
        if forward:
            y_s[pl.ds(r0, q), :] = y
        else:
            y_s[pl.ds(r0, q), :] += y

    def fwd(c, carry):
        chunk(c, True)
        return carry

    def bwd(c, carry):
        chunk(n_chunks - 1 - c, False)
        return carry

    lax.fori_loop(0, n_chunks, fwd, 0)
    lax.fori_loop(0, n_chunks, bwd, 0)
    o_ref[...] = y_s[...] + xs_s[...] * dsk_ref[...]


def _ssd_core(p3, conv_w, conv_b, dskip, dt_rows, cs_rows, cs_cols):
    b, l, _ = p3.shape
    q = SSD_CHUNK
    nc = l // q
    gw = SSD_HEADS_PER_GROUP * SSD_HEAD_DIM
    ns = SSD_D_STATE
    x0 = SSD_D_INNER // gw
    b0 = 2 * SSD_D_INNER // ns
    c0 = b0 + SSD_N_GROUPS
    wb0 = SSD_D_INNER // ns
    wc0 = wb0 + SSD_N_GROUPS
    rows_spec = pl.BlockSpec((None, None, nc, 8, q), lambda i, g: (i, g, 0, 0, 0))
    return pl.pallas_call(
        functools.partial(_ssd_core_body, q=q, n_chunks=nc),
        grid=(b, SSD_N_GROUPS),
        in_specs=[pl.BlockSpec((None, l, gw), lambda i, g: (i, 0, x0 + g)),
                  pl.BlockSpec((None, l, ns), lambda i, g: (i, 0, b0 + g)),
                  pl.BlockSpec((None, l, ns), lambda i, g: (i, 0, c0 + g)),
                  pl.BlockSpec((CONV_K, gw), lambda i, g: (0, g)),
                  pl.BlockSpec((CONV_K, ns), lambda i, g: (0, wb0 + g)),
                  pl.BlockSpec((CONV_K, ns), lambda i, g: (0, wc0 + g)),
                  pl.BlockSpec((1, gw), lambda i, g: (0, g)),
                  pl.BlockSpec((1, ns), lambda i, g: (0, wb0 + g)),
                  pl.BlockSpec((1, ns), lambda i, g: (0, wc0 + g)),
                  pl.BlockSpec((1, gw), lambda i, g: (0, g)),
                  rows_spec, rows_spec,
                  pl.BlockSpec((None, None, l, 8), lambda i, g: (i, g, 0, 0))],
        out_specs=pl.BlockSpec((None, l, gw), lambda i, g: (i, 0, g)),
        out_shape=jax.ShapeDtypeStruct((b, l, SSD_D_INNER), F32),
        scratch_shapes=[pltpu.VMEM((l, gw), F32), pltpu.VMEM((l, ns), F32), pltpu.VMEM((l, ns), BF16),
                        pltpu.VMEM((l, gw), F32), pltpu.VMEM((2 * SSD_HEADS_PER_GROUP, ns, SSD_HEAD_DIM), F32)],
        compiler_params=_cparams("parallel", "parallel"),
        name="ssd_core",
    )(p3, p3, p3, conv_w, conv_w, conv_w, conv_b, conv_b, conv_b, dskip, dt_rows, cs_rows, cs_cols)


def _ssd_out_body(y_ref, z_ref, nw_ref, w_ref, x_ref, o_ref):
    yz = y_ref[...] * _silu(z_ref[...])
    o_ref[...] = x_ref[...] + jnp.dot(_rms(yz, nw_ref[...]).astype(BF16), w_ref[...], preferred_element_type=F32)


def _ssd_out(y, p, norm_w, w_out, x):
    t, d = x.shape
    tm = TOKEN_TILE
    di = SSD_D_INNER
    return pl.pallas_call(
        _ssd_out_body,
        grid=(t // tm,),
        in_specs=[pl.BlockSpec((tm, di), lambda i: (i, 0)),
                  pl.BlockSpec((tm, di), lambda i: (i, 0)),
                  _resident((1, di)), _resident(w_out.shape),
                  pl.BlockSpec((tm, d), lambda i: (i, 0))],
        out_specs=pl.BlockSpec((tm, d), lambda i: (i, 0)),
        out_shape=jax.ShapeDtypeStruct((t, d), F32),
        compiler_params=_cparams("parallel"),
        name="ssd_out",
    )(y, p, norm_w, w_out, x)


def _group_rows_cols(a, b, l, nc, q):
    g = a.shape[2]
    cols = jnp.transpose(a, (0, 2, 1, 3))
    rows = jnp.transpose(a.reshape(b, nc, q, g, 8), (0, 3, 1, 4, 2))
    return rows, cols


def _ssd_mixer(x, b, l, g_norm, w_in, conv_w, conv_b, a_log, dt_bias, d_skip, norm_w, w_out):
    t = b * l
    q = SSD_CHUNK
    nc = l // q
    p = _proj(x, g_norm, w_in, 1280)
    dt, cs = _ssd_prep(p, dt_bias, a_log)
    ng, nh = SSD_N_GROUPS, SSD_HEADS_PER_GROUP

    def regroup(a):
        a = a[:, :2 * ng * nh].reshape(b, l, 2, ng, nh)
        return jnp.transpose(a, (0, 1, 3, 2, 4)).reshape(b, l, ng, 2 * nh)

    dt_rows, _ = _group_rows_cols(regroup(dt), b, l, nc, q)
    cs_rows, cs_cols = _group_rows_cols(regroup(cs), b, l, nc, q)
    y = _ssd_core(p.reshape(b, l, PROJ_PAD), conv_w, conv_b, d_skip, dt_rows, cs_rows, cs_cols)
    return _ssd_out(y.reshape(t, SSD_D_INNER), p, norm_w, w_out, x)


def _gdn_prep_body(p_ref, bias_ref, alog_ref, o_ref, *, cs):
    raw = p_ref[:, 0:LANES]
    beta = _sigmoid(raw)
    g = -jnp.exp(alog_ref[...]) * _softplus(raw + bias_ref[...])
    n = LANES
    li = lax.broadcasted_iota(jnp.int32, (n, n), 0)
    si = lax.broadcasted_iota(jnp.int32, (n, n), 1)
    sh = int(math.log2(cs))
    same = jnp.right_shift(li, sh) == jnp.right_shift(si, sh)
    tril = (same & (li >= si)).astype(F32)
    triu = (same & (li <= si)).astype(F32)
    lane = lax.broadcasted_iota(jnp.int32, (n, LANES), 1)
    hv = GDN_N_V_HEADS
    for c in range(raw.shape[0] // n):
        blk = g[c * n:(c + 1) * n, :]
        o_ref[c * n:(c + 1) * n, :] = jnp.where(
            lane < hv, beta[c * n:(c + 1) * n, :],
            jnp.where(lane < 2 * hv, _dot_f32(tril, blk), _dot_f32(triu, blk)))


def _gdn_prep(p, bias, alog):
    t = p.shape[0]
    tm = TOKEN_TILE
    return pl.pallas_call(
        functools.partial(_gdn_prep_body, cs=GDN_CHUNK),
        grid=(t // tm,),
        in_specs=[pl.BlockSpec((tm, 2 * LANES), lambda i: (i, (PROJ_PAD - 2 * LANES) // (2 * LANES))),
                  _resident((1, LANES)), _resident((1, LANES))],
        out_specs=pl.BlockSpec((tm, LANES), lambda i: (i, 0)),
        out_shape=jax.ShapeDtypeStruct((t, LANES), F32),
        compiler_params=_cparams("parallel"),
        name="gdn_prep",
    )(p, bias, alog)


def _gdn_core_body(q_ref, k_ref, v_ref, wq_ref, wk_ref, wv_ref, bq_ref, bk_ref, bv_ref, rows_ref, cols_ref,
                   o_ref, q_s, k_s, v_s, y_s, st_ref, *, cs, n_chunks):
    hd = GDN_HEAD
    qn = _dwconv_silu(q_ref[...], wq_ref[...], bq_ref[...])
    qn = qn * lax.rsqrt(jnp.sum(qn * qn, axis=-1, keepdims=True) + 1e-6)
    q_s[...] = qn * (hd ** -0.5)
    kn = _dwconv_silu(k_ref[...], wk_ref[...], bk_ref[...])
    k_s[...] = kn * lax.rsqrt(jnp.sum(kn * kn, axis=-1, keepdims=True) + 1e-6)
    v_s[...] = _dwconv_silu(v_ref[...], wv_ref[...], bv_ref[...])
    st_ref[...] = jnp.zeros_like(st_ref)

    def chunk(c, forward):
        r0 = pl.multiple_of(c * cs, cs)
        incl, strict = _tri_masks(cs, forward)
        qc = q_s[pl.ds(r0, cs), :]
        kc = k_s[pl.ds(r0, cs), :]
        vc = v_s[pl.ds(r0, cs), :]
        kk = _dot_nt(kc, kc)
        qk0 = _dot_nt(qc, kc)
        kt = kc.T
        cols = cols_ref[pl.ds(r0, cs), :]
        rows = rows_ref[c]
        outs = []
        for e in range(2):
            gi = (2 if forward else 4) + e
            si = (0 if forward else 2) + e
            bcol = cols[:, e:e + 1]
            gcol = cols[:, gi:gi + 1]
            grow = rows[gi:gi + 1, :]
            dec = jnp.where(incl, jnp.exp(gcol - grow), 0.0)
            nmat = jnp.where(strict, -(bcol * kk * dec), 0.0)
            eg = jnp.exp(gcol)
            rhs = jnp.concatenate([vc[:, e * hd:(e + 1) * hd] * bcol, kc * (bcol * eg)], axis=1)
            sol = _tri_apply(nmat, rhs, cs)
            u = sol[:, :hd]
            w = sol[:, hd:]
            qkm = jnp.where(incl, qk0 * dec, 0.0)
            gl = grow[:, cs - 1:cs] if forward else grow[:, 0:1]
            s = st_ref[si]
            v_new = u - _dot(w, s)
            outs.append(_dot(qc * eg, s) + _dot(qkm, v_new))
            st_ref[si] = s * jnp.exp(gl) + _dot(kt * jnp.exp(gl - grow), v_new)
        o = jnp.concatenate(outs, axis=1)
        if forward:
            y_s[pl.ds(r0, cs), :] = o
        else:
            y_s[pl.ds(r0, cs), :] += o

    def fwd(c, carry):
        chunk(c, True)
        return carry

    def bwd(c, carry):
        chunk(n_chunks - 1 - c, False)
        return carry

    lax.fori_loop(0, n_chunks, fwd, 0)
    lax.fori_loop(0, n_chunks, bwd, 0)
    o_ref[...] = y_s[...]


def _gdn_core(p3, conv_w, conv_b, rows, cols):
    b, l, _ = p3.shape
    cs = GDN_CHUNK
    nc = l // cs
    hd = GDN_HEAD
    vw = 2 * hd
    k0 = GDN_QK_DIM // hd
    v0 = 2 * GDN_QK_DIM // vw
    return pl.pallas_call(
        functools.partial(_gdn_core_body, cs=cs, n_chunks=nc),
        grid=(b, GDN_N_K_HEADS),
        in_specs=[pl.BlockSpec((None, l, hd), lambda i, j: (i, 0, j)),
                  pl.BlockSpec((None, l, hd), lambda i, j: (i, 0, k0 + j)),
                  pl.BlockSpec((None, l, vw), lambda i, j: (i, 0, v0 + j)),
                  pl.BlockSpec((CONV_K, hd), lambda i, j: (0, j)),
                  pl.BlockSpec((CONV_K, hd), lambda i, j: (0, k0 + j)),
                  pl.BlockSpec((CONV_K, vw), lambda i, j: (0, v0 + j)),
                  pl.BlockSpec((1, hd), lambda i, j: (0, j)),
                  pl.BlockSpec((1, hd), lambda i, j: (0, k0 + j)),
                  pl.BlockSpec((1, vw), lambda i, j: (0, v0 + j)),
                  pl.BlockSpec((None, None, nc, 8, cs), lambda i, j: (i, j, 0, 0, 0)),
                  pl.BlockSpec((None, None, l, 8), lambda i, j: (i, j, 0, 0))],
        out_specs=pl.BlockSpec((None, l, vw), lambda i, j: (i, 0, j)),
        out_shape=jax.ShapeDtypeStruct((b, l, GDN_V_DIM), F32),
        scratch_shapes=[pltpu.VMEM((l, hd), F32), pltpu.VMEM((l, hd), F32), pltpu.VMEM((l, vw), F32),
                        pltpu.VMEM((l, vw), F32), pltpu.VMEM((4, hd, hd), F32)],
        compiler_params=_cparams("parallel", "parallel"),
        name="gdn_core",
    )(p3, p3, p3, conv_w, conv_w, conv_w, conv_b, conv_b, conv_b, rows, cols)


def _gdn_out_body(o_ref, z_ref, nw_ref, w_ref, x_ref, out_ref, h_s):
    hd = GDN_HEAD
    nw = nw_ref[...]
    for e in range(GDN_N_V_HEADS):
        oe = o_ref[:, e * hd:(e + 1) * hd]
        h_s[:, e * hd:(e + 1) * hd] = (_rms(oe, nw) * _silu(z_ref[:, e * hd:(e + 1) * hd])).astype(BF16)
    out_ref[...] = x_ref[...] + jnp.dot(h_s[...], w_ref[...], preferred_element_type=F32)


def _gdn_out(o, p, norm_w, w_out, x):
    t, d = x.shape
    tm = TOKEN_TILE
    dv = GDN_V_DIM
    return pl.pallas_call(
        _gdn_out_body,
        grid=(t // tm,),
        in_specs=[pl.BlockSpec((tm, dv), lambda i: (i, 0)),
                  pl.BlockSpec((tm, dv), lambda i: (i, 2 * GDN_QK_DIM // dv + 1)),
                  _resident((1, GDN_HEAD)), _resident(w_out.shape),
                  pl.BlockSpec((tm, d), lambda i: (i, 0))],
        out_specs=pl.BlockSpec((tm, d), lambda i: (i, 0)),
        out_shape=jax.ShapeDtypeStruct((t, d), F32),
        scratch_shapes=[pltpu.VMEM((tm, dv), BF16)],
        compiler_params=_cparams("parallel"),
        name="gdn_out",
    )(o, p, norm_w, w_out, x)


def _gdn_mixer(x, b, l, g_norm, w_in, conv_w, conv_b, a_log, dt_bias, norm_w, w_out):
    t = b * l
    cs = GDN_CHUNK
    nc = l // cs
    p = _proj(x, g_norm, w_in, 1280)
    gt = _gdn_prep(p, dt_bias, a_log).reshape(b, l, LANES)
    hk, hv = GDN_N_K_HEADS, GDN_N_V_HEADS
    parts = [gt[..., i * hv:(i + 1) * hv].reshape(b, l, hk, 2) for i in range(3)]
    a = jnp.concatenate(parts + [jnp.zeros((b, l, hk, 2), F32)], axis=-1)
    rows, cols = _group_rows_cols(a, b, l, nc, cs)
    o = _gdn_core(p.reshape(b, l, PROJ_PAD), conv_w, conv_b, rows, cols)
    return _gdn_out(o.reshape(t, GDN_V_DIM), p, norm_w, w_out, x)


def _att_core_body(q_ref, k_ref, v_ref, cos_ref, sin_ref, sink_ref, o_ref, q_s, k_s, v_s, *, win, n_blocks, seq):
    dh = ATT_HEAD_DIM
    cos = cos_ref[...]
    sin = sin_ref[...]

    def rope(x):
        w = x.shape[1]
        lane = lax.broadcasted_iota(jnp.int32, x.shape, 1)
        first = jnp.bitwise_and(lane, dh - 1) < dh // 2
        partner = jnp.where(first, pltpu.roll(x, w - dh // 2, axis=1), pltpu.roll(x, dh // 2, axis=1))
        reps = w // LANES
        c = jnp.concatenate([cos] * reps, axis=1) if reps > 1 else cos
        s = jnp.concatenate([sin] * reps, axis=1) if reps > 1 else sin
        return x * c + partner * s

    q_s[...] = rope(q_ref[...]).astype(BF16)
    zeros = jnp.zeros((win, LANES), BF16)
    k_s[pl.ds(0, win), :] = zeros
    k_s[pl.ds(win + seq, win), :] = zeros
    v_s[pl.ds(0, win), :] = zeros
    v_s[pl.ds(win + seq, win), :] = zeros
    k_s[pl.ds(win, seq), :] = rope(k_ref[...]).astype(BF16)
    v_s[pl.ds(win, seq), :] = v_ref[...].astype(BF16)
    sinks = sink_ref[...]
    n_heads = q_ref.shape[1] // dh
    rep = ATT_N_HEADS // ATT_N_KV_HEADS

    def block(n, carry):
        r0 = pl.multiple_of(n * win, win)
        qb = q_s[pl.ds(r0, win), :]
        kw = k_s[pl.ds(r0, 3 * win), :]
        vw = v_s[pl.ds(r0, 3 * win), :]
        qpos = r0 + lax.broadcasted_iota(jnp.int32, (win, 3 * win), 0)
        kpos = r0 - win + lax.broadcasted_iota(jnp.int32, (win, 3 * win), 1)
        valid = (jnp.abs(kpos - qpos) <= win) & (kpos >= 0) & (kpos < seq)
        outs = []
        for h in range(n_heads):
            kv = h // rep
            s = _dot_nt(qb[:, h * dh:(h + 1) * dh], kw[:, kv * dh:(kv + 1) * dh]) * (dh ** -0.5)
            s = jnp.where(valid, s, -jnp.inf)
            sink = sinks[:, h:h + 1]
            m = jnp.maximum(jnp.max(s, axis=-1, keepdims=True), sink)
            e = jnp.exp(s - m)
            p = e / (jnp.sum(e, axis=-1, keepdims=True) + jnp.exp(sink - m))
            outs.append(_dot(p, vw[:, kv * dh:(kv + 1) * dh]))
        o_ref[pl.ds(r0, win), :] = jnp.concatenate(outs, axis=1).astype(o_ref.dtype)
        return carry

    lax.fori_loop(0, n_blocks, block, 0)


def _att_core(p3, cos2, sin2, sinks3):
    b, l, _ = p3.shape
    win = ATT_WINDOW
    dh = ATT_HEAD_DIM
    steps = ATT_N_KV_HEADS * dh // LANES
    qw = ATT_N_HEADS * dh // steps
    k0 = ATT_N_HEADS * dh // LANES
    v0 = k0 + steps
    return pl.pallas_call(
        functools.partial(_att_core_body, win=win, n_blocks=l // win, seq=l),
        grid=(b, steps),
        in_specs=[pl.BlockSpec((None, l, qw), lambda i, j: (i, 0, j)),
                  pl.BlockSpec((None, l, LANES), lambda i, j: (i, 0, k0 + j)),
                  pl.BlockSpec((None, l, LANES), lambda i, j: (i, 0, v0 + j)),
                  _resident((l, LANES)), _resident((l, LANES)),
                  pl.BlockSpec((None, 1, qw // dh), lambda i, j: (j, 0, 0))],
        out_specs=pl.BlockSpec((None, l, qw), lambda i, j: (i, 0, j)),
        out_shape=jax.ShapeDtypeStruct((b, l, ATT_N_HEADS * dh), BF16),
        scratch_shapes=[pltpu.VMEM((l, qw), BF16), pltpu.VMEM((l + 2 * win, LANES), BF16),
                        pltpu.VMEM((l + 2 * win, LANES), BF16)],
        compiler_params=_cparams("parallel", "parallel"),
        name="att_core",
    )(p3, p3, p3, cos2, sin2, sinks3)


def _res_proj_body(a_ref, w_ref, x_ref, o_ref):
    o_ref[...] = x_ref[...] + jnp.dot(a_ref[...], w_ref[...], preferred_element_type=F32)


def _res_proj(a, w, x):
    t, d = x.shape
    k = a.shape[1]
    tm = TOKEN_TILE
    return pl.pallas_call(
        _res_proj_body,
        grid=(t // tm,),
        in_specs=[pl.BlockSpec((tm, k), lambda i: (i, 0)), _resident(w.shape),
                  pl.BlockSpec((tm, d), lambda i: (i, 0))],
        out_specs=pl.BlockSpec((tm, d), lambda i: (i, 0)),
        out_shape=jax.ShapeDtypeStruct((t, d), F32),
        compiler_params=_cparams("parallel"),
        name="res_proj",
    )(a, w, x)


def _att_mixer(x, b, l, g_norm, w_qkv, cos2, sin2, sinks3, w_out):
    t = b * l
    p = _proj(x, g_norm, w_qkv, w_qkv.shape[1] // 2)
    o = _att_core(p.reshape(b, l, -1), cos2, sin2, sinks3)
    return _res_proj(o.reshape(t, -1), w_out, x)


def _rwkv_prep_body(x_ref, xp_ref, xn_ref, g_ref, mu_ref, wr_ref, wk_ref, wv_ref, w1_ref, w2_ref, w0_ref,
                    a0_ref, a1_ref, a2_ref, g1_ref, g2_ref, kk_ref, ka_ref,
                    r_o, k_o, v_o, kk_o, a_o, lw0_o, lw1_o, gate_o, *, n_tiles):
    i = pl.program_id(1)
    g = g_ref[...]
    u = _rms(x_ref[...], g)
    tl = u.shape[0]
    prev_row = jnp.where(i > 0, _rms(xp_ref[...], g)[7:8, :], 0.0)
    next_row = jnp.where(i < n_tiles - 1, _rms(xn_ref[...], g)[0:1, :], 0.0)
    row = lax.broadcasted_iota(jnp.int32, u.shape, 0)
    u_prev = jnp.where(row == 0, prev_row, pltpu.roll(u, 1, axis=0))
    u_next = jnp.where(row == tl - 1, next_row, pltpu.roll(u, tl - 1, axis=0))
    xx = 0.5 * (u_prev + u_next) - u
    mu = mu_ref[...]

    def mix(s):
        return (u + xx * mu[s:s + 1, :]).astype(BF16)

    r_o[...] = jnp.dot(mix(0), wr_ref[...], preferred_element_type=F32)
    k = jnp.dot(mix(1), wk_ref[...], preferred_element_type=F32)
    v_o[...] = jnp.dot(mix(2), wv_ref[...], preferred_element_type=F32)
    wl = jnp.tanh(jnp.dot(mix(3), w1_ref[...], preferred_element_type=F32))
    lora = w2_ref.shape[1]
    w0 = w0_ref[...]
    for d, out in enumerate((lw0_o, lw1_o)):
        z = w0[d:d + 1, :] + _dot(wl[:, d * lora:(d + 1) * lora], w2_ref[d])
        out[...] = -jnp.exp(-_softplus(-z) - 0.5)
    a = _sigmoid(a0_ref[...] + _dot(jnp.dot(mix(4), a1_ref[...], preferred_element_type=F32), a2_ref[...]))
    gate_o[...] = _dot(_sigmoid(jnp.dot(mix(5), g1_ref[...], preferred_element_type=F32)), g2_ref[...])
    a_o[...] = a
    kk_o[...] = k * kk_ref[...]
    k_o[...] = k * (1.0 + (a - 1.0) * ka_ref[...])


def _rwkv_prep(x3, g, mu, wr, wk, wv, w1, w2, w0, a0, a1, a2, g1, g2, k_k, k_a):
    b, l, d = x3.shape
    tl = 256
    nt = l // tl
    x4 = x3.reshape(b, l // 8, 8, d)
    tile = pl.BlockSpec((None, tl, d), lambda i, j: (i, j, 0))
    consts = [g, mu, wr, wk, wv, w1, w2, w0, a0, a1, a2, g1, g2, k_k, k_a]
    return pl.pallas_call(
        functools.partial(_rwkv_prep_body, n_tiles=nt),
        grid=(b, nt),
        in_specs=[tile,
                  pl.BlockSpec((None, None, 8, d), lambda i, j: (i, jnp.maximum(j * (tl // 8) - 1, 0), 0, 0)),
                  pl.BlockSpec((None, None, 8, d),
                               lambda i, j: (i, jnp.minimum((j + 1) * (tl // 8), l // 8 - 1), 0, 0))]
                 + [_resident(c.shape) for c in consts],
        out_specs=[tile] * 8,
        out_shape=[jax.ShapeDtypeStruct((b, l, d), F32)] * 8,
        compiler_params=_cparams("parallel", "parallel"),
        name="rwkv_prep",
    )(x3, x4, x4, *consts)


def _rwkv_core_body(r_ref, k_ref, v_ref, kk_ref, a_ref, lw0_ref, lw1_ref, rk_ref, lnw_ref, lnb_ref, o_ref,
                    kk_s, kb_s, y_s, st_ref, *, cs, n_chunks):
    n = RWKV_HEAD
    kk = kk_ref[...]
    kk = kk * lax.rsqrt(_seg_sum2(kk * kk) + 1e-6)
    kk_s[...] = kk
    kb_s[...] = kk * a_ref[...]
    st_ref[...] = jnp.zeros_like(st_ref)

    def chunk(c, forward):
        r0 = pl.multiple_of(c * cs, cs)
        incl, strict = _tri_masks(cs, forward)
        lw = (lw0_ref if forward else lw1_ref)[pl.ds(r0, cs), :]
        cum = _dot_f32(incl.astype(F32), lw)
        tot = cum[cs - 1:cs, :] if forward else cum[0:1, :]
        rc = r_ref[pl.ds(r0, cs), :]
        kc = k_ref[pl.ds(r0, cs), :]
        vc = v_ref[pl.ds(r0, cs), :]
        kkc = kk_s[pl.ds(r0, cs), :]
        kbc = kb_s[pl.ds(r0, cs), :]
        e_neg = jnp.exp(-cum)
        e_end = jnp.exp(tot - cum)
        p_end = jnp.exp(tot)
        a_hat = -kkc * jnp.exp(cum - lw)
        r_hat = rc * jnp.exp(cum)
        b_hat = kbc * e_neg
        k_hat = kc * e_neg
        b_end = kbc * e_end
        k_end = kc * e_end
        outs = []
        for e in range(2):
            sl = slice(e * n, (e + 1) * n)
            si = (0 if forward else 2) + e
            lhs = jnp.concatenate([a_hat[:, sl], r_hat[:, sl]], axis=0)
            rhs = jnp.concatenate([b_hat[:, sl], k_hat[:, sl]], axis=0)
            gram = lax.dot_general(lhs, rhs, (((1,), (1,)), ((), ())), precision=lax.Precision.HIGHEST,
                                   preferred_element_type=F32)
            n_ab = jnp.where(strict, gram[:cs, :cs], 0.0)
            a_ak = jnp.where(strict, gram[:cs, cs:], 0.0)
            a_rb = jnp.where(incl, gram[cs:, :cs], 0.0)
            a_rk = jnp.where(incl, gram[cs:, cs:], 0.0)
            s = st_ref[si]
            ve = vc[:, sl]
            u = _tri_apply(n_ab, _dot_nt(a_hat[:, sl], s) + _dot(a_ak, ve), cs)
            outs.append(_dot_nt(r_hat[:, sl], s) + _dot(a_rb, u) + _dot(a_rk, ve))
            st_ref[si] = s * p_end[:, sl] + _dot_tn(u, b_end[:, sl]) + _dot_tn(ve, k_end[:, sl])
        o = jnp.concatenate(outs, axis=1)
        if forward:
            y_s[pl.ds(r0, cs), :] = o
        else:
            y_s[pl.ds(r0, cs), :] += o

    def fwd(c, carry):
        chunk(c, True)
        return carry

    def bwd(c, carry):
        chunk(n_chunks - 1 - c, False)
        return carry

    lax.fori_loop(0, n_chunks, fwd, 0)
    lax.fori_loop(0, n_chunks, bwd, 0)
    o = y_s[...]
    mean = _seg_sum2(o) * (1.0 / n)
    cen = o - mean
    var = _seg_sum2(cen * cen) * (1.0 / n)
    y = cen * lax.rsqrt(var + RWKV_LN_EPS) * lnw_ref[...] + lnb_ref[...]
    bonus = _seg_sum2(r_ref[...] * k_ref[...] * rk_ref[...]) * v_ref[...]
    o_ref[...] = y + bonus


def _rwkv_core(r, k, v, kk, a, lw0, lw1, r_k, lnw, lnb):
    b, l, d = r.shape
    cs = RWKV_CHUNK
    pair = pl.BlockSpec((None, l, LANES), lambda i, j: (i, 0, j))
    vec = pl.BlockSpec((1, LANES), lambda i, j: (0, j))
    return pl.pallas_call(
        functools.partial(_rwkv_core_body, cs=cs, n_chunks=l // cs),
        grid=(b, d // LANES),
        in_specs=[pair] * 7 + [vec] * 3,
        out_specs=pair,
        out_shape=jax.ShapeDtypeStruct((b, l, d), F32),
        scratch_shapes=[pltpu.VMEM((l, LANES), F32), pltpu.VMEM((l, LANES), F32), pltpu.VMEM((l, LANES), F32),
                        pltpu.VMEM((4, RWKV_HEAD, RWKV_HEAD), F32)],
        compiler_params=_cparams("parallel", "parallel"),
        name="rwkv_core",
    )(r, k, v, kk, a, lw0, lw1, r_k, lnw, lnb)


def _gate_proj_body(y_ref, gate_ref, w_ref, x_ref, o_ref):
    o_ref[...] = x_ref[...] + jnp.dot((y_ref[...] * gate_ref[...]).astype(BF16), w_ref[...],
                                      preferred_element_type=F32)


def _gate_proj(y, gate, w, x):
    t, d = x.shape
    tm = TOKEN_TILE
    tile = pl.BlockSpec((tm, d), lambda i: (i, 0))
    return pl.pallas_call(
        _gate_proj_body,
        grid=(t // tm,),
        in_specs=[tile, tile, _resident(w.shape), tile],
        out_specs=tile,
        out_shape=jax.ShapeDtypeStruct((t, d), F32),
        compiler_params=_cparams("parallel"),
        name="rwkv_out",
    )(y, gate, w, x)


def _rwkv_mixer(x, b, l, g_norm, mu, wr, wk, wv, w1, w2, w0, a0, a1, a2, g1, g2, k_k, k_a, r_k, lnw, lnb, w_out):
    t, d = x.shape
    r, k, v, kk, a, lw0, lw1, gate = _rwkv_prep(x.reshape(b, l, d), g_norm, mu, wr, wk, wv, w1, w2, w0,
                                                a0, a1, a2, g1, g2, k_k, k_a)
    y = _rwkv_core(r, k, v, kk, a, lw0, lw1, r_k, lnw, lnb)
    return _gate_proj(y.reshape(t, d), gate.reshape(t, d), w_out, x)


def _row(v):
    return v.reshape(1, -1).astype(F32)


def _pad_cols(w, n):
    return jnp.pad(w, ((0, 0), (0, n - w.shape[1])))


def _pad_row(v, n):
    v = v.reshape(1, -1).astype(F32)
    return jnp.pad(v, ((0, 0), (0, n - v.shape[1])))


def _ffn_weights(w_gu, w_down):
    d, f2 = w_gu.shape
    f = f2 // 2
    nck = f // FFN_CHUNK
    wg3 = jnp.transpose(w_gu[:, :f].reshape(d, nck, FFN_CHUNK), (1, 0, 2)).astype(BF16)
    wu3 = jnp.transpose(w_gu[:, f:].reshape(d, nck, FFN_CHUNK), (1, 0, 2)).astype(BF16)
    wd3 = w_down.reshape(nck, FFN_CHUNK, d).astype(BF16)
    return wg3, wu3, wd3


def _rope_tables(l):
    half = ATT_HEAD_DIM // 2
    inv_freq = ROPE_THETA ** (-jnp.arange(half, dtype=F32) / half)
    ang = jnp.arange(l).astype(F32)[:, None] * inv_freq[None, :]
    cos, sin = jnp.cos(ang), jnp.sin(ang)
    reps = LANES // ATT_HEAD_DIM
    cos2 = jnp.tile(jnp.concatenate([cos, cos], axis=1), (1, reps))
    sin2 = jnp.tile(jnp.concatenate([-sin, sin], axis=1), (1, reps))
    return cos2, sin2


def kernel(x_prompt, x_sample, ffn1_norm, ffn1_w_gu, ffn1_w_down, mix_norm, ffn2_norm, ffn2_w_gu, ffn2_w_down, ssd_w_in, ssd_conv_w, ssd_conv_b, ssd_a_log, ssd_dt_bias, ssd_d, ssd_norm, ssd_w_out, gdn_w_in, gdn_conv_w, gdn_conv_b, gdn_a_log, gdn_dt_bias, gdn_norm, gdn_w_out, att_w_qkv, att_sinks, att_w_out, rwkv_x_mu, rwkv_w_rkv, rwkv_w0, rwkv_w1, rwkv_w2, rwkv_a0, rwkv_a1, rwkv_a2, rwkv_g1, rwkv_g2, rwkv_k_k, rwkv_k_a, rwkv_r_k, rwkv_lnx_w, rwkv_lnx_b, rwkv_w_out, final_norm):
    depth = ffn1_norm.shape[0]
    ffn1 = [_ffn_weights(ffn1_w_gu[i], ffn1_w_down[i]) for i in range(depth)]
    ffn2 = [_ffn_weights(ffn2_w_gu[i], ffn2_w_down[i]) for i in range(depth)]
    fin_g = _row(final_norm)
    hv = GDN_N_V_HEADS
    att_steps = ATT_N_KV_HEADS * ATT_HEAD_DIM // LANES

    def mixer(h, b, l, i):
        m, j = i % N_MIXERS, i // N_MIXERS
        g = _row(mix_norm[i])
        if m == 0:
            return _ssd_mixer(h, b, l, g, _pad_cols(ssd_w_in[j], PROJ_PAD).astype(BF16), ssd_conv_w[j],
                              _row(ssd_conv_b[j]), _pad_row(ssd_a_log[j], LANES), _pad_row(ssd_dt_bias[j], LANES),
                              _row(jnp.repeat(ssd_d[j], SSD_HEAD_DIM)), _row(ssd_norm[j]), ssd_w_out[j].astype(BF16))
        if m == 1:
            lead = jnp.zeros((1, hv), F32)
            alog = jnp.pad(jnp.concatenate([lead, _row(gdn_a_log[j])], axis=1), ((0, 0), (0, LANES - 3 * hv)))
            bias = jnp.pad(jnp.concatenate([lead, _row(gdn_dt_bias[j])], axis=1), ((0, 0), (0, LANES - 3 * hv)))
            return _gdn_mixer(h, b, l, g, _pad_cols(gdn_w_in[j], PROJ_PAD).astype(BF16), gdn_conv_w[j],
                              _row(gdn_conv_b[j]), alog, bias, _row(gdn_norm[j]), gdn_w_out[j].astype(BF16))
        if m == 2:
            cos2, sin2 = _rope_tables(l)
            return _att_mixer(h, b, l, g, att_w_qkv[j].astype(BF16), cos2, sin2,
                              att_sinks[j].astype(F32).reshape(att_steps, 1, -1), att_w_out[j].astype(BF16))
        w1 = jnp.concatenate([rwkv_w1[j, 0], rwkv_w1[j, 1]], axis=1).astype(BF16)
        return _rwkv_mixer(h, b, l, g, rwkv_x_mu[j], rwkv_w_rkv[j, 0].astype(BF16), rwkv_w_rkv[j, 1].astype(BF16),
                           rwkv_w_rkv[j, 2].astype(BF16), w1, rwkv_w2[j].astype(BF16), rwkv_w0[j],
                           _row(rwkv_a0[j]), rwkv_a1[j].astype(BF16), rwkv_a2[j].astype(BF16),
                           rwkv_g1[j].astype(BF16), rwkv_g2[j].astype(BF16), _row(rwkv_k_k[j]), _row(rwkv_k_a[j]),
                           _row(rwkv_r_k[j]), _row(rwkv_lnx_w[j]), _row(rwkv_lnx_b[j]), rwkv_w_out[j].astype(BF16))

    def trunk(x3):
        b, l, d = x3.shape
        x = x3.reshape(b * l, d)
        for i in range(depth):
            x = _ffn(x, _row(ffn1_norm[i]), *ffn1[i], fin_g, False)
            x = mixer(x, b, l, i)
            x = _ffn(x, _row(ffn2_norm[i]), *ffn2[i], fin_g, i == depth - 1)
        return x.reshape(b, l, d)

    return (trunk(x_prompt), trunk(x_sample))
```

```python
import functools
import math

import jax
import jax.numpy as jnp
from jax import lax
from jax.experimental import pallas as pl
from jax.experimental.pallas import tpu as pltpu

F32 = jnp.float32
BF16 = jnp.bfloat16

V7X_VMEM_LIMIT_BYTES = 56 * 1024 * 1024
LANES = 128

RMS_EPS = 1e-6
CONV_K = 5
N_MIXERS = 4

SSD_HEAD_DIM = 64
SSD_N_GROUPS = 8
SSD_HEADS_PER_GROUP = 4
SSD_D_STATE = 128
SSD_CHUNK = 128
SSD_D_INNER = 2048

GDN_N_K_HEADS = 8
GDN_N_V_HEADS = 16
GDN_HEAD = 128
GDN_CHUNK = 64
GDN_QK_DIM = 1024
GDN_V_DIM = 2048

ATT_N_HEADS = 16
ATT_N_KV_HEADS = 4
ATT_HEAD_DIM = 64
ATT_WINDOW = 128
ROPE_THETA = 10000.0

RWKV_HEAD = 64
RWKV_CHUNK = 64
RWKV_LN_EPS = 64e-5

PROJ_PAD = 6400

TOKEN_TILE = 512
FFN_CHUNK = 256
FACTOR_CHUNKS = 4


def _cparams(*sem):
    return pltpu.CompilerParams(dimension_semantics=sem, vmem_limit_bytes=V7X_VMEM_LIMIT_BYTES)


def _resident(shape):
    nd = len(shape)
    return pl.BlockSpec(shape, lambda *_: (0,) * nd, pipeline_mode=pl.Buffered(1))


def _sigmoid(x):
    return 1.0 / (1.0 + jnp.exp(-x))


def _silu(x):
    return x * _sigmoid(x)


def _softplus(x):
    return jnp.maximum(x, 0.0) + jnp.log1p(jnp.exp(-jnp.abs(x)))


def _rms(x, g):
    return x * lax.rsqrt(jnp.mean(x * x, axis=-1, keepdims=True) + RMS_EPS) * g


def _dot(a, b):
    return jnp.dot(a.astype(BF16), b.astype(BF16), preferred_element_type=F32)


def _dot_nt(a, b):
    return lax.dot_general(a.astype(BF16), b.astype(BF16), (((1,), (1,)), ((), ())), preferred_element_type=F32)


def _dot_f32(a, b):
    return jnp.dot(a, b, precision=lax.Precision.HIGHEST, preferred_element_type=F32)


def _tri_apply_many(ns, xs, size):
    steps = int(math.log2(size))
    ms = list(ns)
    xs = list(xs)
    for s in range(steps):
        prods = [_dot(m, x) for m, x in zip(ms, xs)]
        if s + 1 < steps:
            ms = [_dot(m, m) for m in ms]
        xs = [x + p for x, p in zip(xs, prods)]
    return xs


def _dwconv_silu(x, w, b):
    n = x.shape[0]
    half = CONV_K // 2

    def taps(xx, masked):
        m = xx.shape[0]
        acc = xx * w[half:half + 1, :] + b
        for k in range(-half, half + 1):
            if k == 0:
                continue
            xr = pltpu.roll(xx, (-k) % m, axis=0)
            if masked:
                row = lax.broadcasted_iota(jnp.int32, xx.shape, 0)
                xr = jnp.where((row + k >= 0) & (row + k < m), xr, 0.0)
            acc = acc + xr * w[k + half:k + half + 1, :]
        return acc

    edge = 8
    head = taps(x[:2 * edge, :], True)[:edge, :]
    tail = taps(x[n - 2 * edge:, :], True)[edge:, :]
    acc = jnp.concatenate([head, taps(x, False)[edge:n - edge, :], tail], axis=0)
    return _silu(acc)


def _seg_sum2(x):
    lane = lax.broadcasted_iota(jnp.int32, x.shape, 1)
    lo = lane < 64
    s_lo = jnp.sum(jnp.where(lo, x, 0.0), axis=1, keepdims=True)
    s_hi = jnp.sum(jnp.where(lo, 0.0, x), axis=1, keepdims=True)
    return jnp.where(lo, s_lo, s_hi)


def _tri_masks(n, forward):
    li = lax.broadcasted_iota(jnp.int32, (n, n), 0)
    si = lax.broadcasted_iota(jnp.int32, (n, n), 1)
    if forward:
        return li >= si, li > si
    return li <= si, li < si


def _ffn_body(x_ref, g_ref, wg_ref, wu_ref, wd_ref, fg_ref, o_ref, xn_ref, acc_ref, *, n_chunks, final):
    x = x_ref[...]
    xn_ref[...] = _rms(x, g_ref[...]).astype(BF16)
    acc_ref[...] = jnp.zeros_like(acc_ref)

    def chunk(c, carry):
        xb = xn_ref[...]
        gate = jnp.dot(xb, wg_ref[c], preferred_element_type=F32)
        up = jnp.dot(xb, wu_ref[c], preferred_element_type=F32)
        h = (_silu(gate) * up).astype(BF16)
        acc_ref[...] += jnp.dot(h, wd_ref[c], preferred_element_type=F32)
        return carry

    lax.fori_loop(0, n_chunks, chunk, 0)
    y = x + 0.5 * acc_ref[...]
    if final:
        y = _rms(y, fg_ref[...])
    o_ref[...] = y


def _ffn(x, g, wg3, wu3, wd3, final_g, final):
    t, d = x.shape
    tm = TOKEN_TILE
    body = functools.partial(_ffn_body, n_chunks=wg3.shape[0], final=final)
    return pl.pallas_call(
        body,
        grid=(t // tm,),
        in_specs=[pl.BlockSpec((tm, d), lambda i: (i, 0)),
                  _resident((1, d)), _resident(wg3.shape), _resident(wu3.shape), _resident(wd3.shape),
                  _resident((1, d))],
        out_specs=pl.BlockSpec((tm, d), lambda i: (i, 0)),
        out_shape=jax.ShapeDtypeStruct((t, d), F32),
        scratch_shapes=[pltpu.VMEM((tm, d), BF16), pltpu.VMEM((tm, d), F32)],
        compiler_params=_cparams("parallel"),
        name="ffn",
    )(x, g, wg3, wu3, wd3, final_g)


def _proj_body(x_ref, g_ref, w_ref, o_ref, xn_ref):
    @pl.when(pl.program_id(1) == 0)
    def _():
        xn_ref[...] = _rms(x_ref[...], g_ref[...]).astype(BF16)

    o_ref[...] = jnp.dot(xn_ref[...], w_ref[...], preferred_element_type=F32)


def _proj(x, g, w, tn):
    t, d = x.shape
    n = w.shape[1]
    tm = TOKEN_TILE
    return pl.pallas_call(
        _proj_body,
        grid=(t // tm, n // tn),
        in_specs=[pl.BlockSpec((tm, d), lambda i, j: (i, 0)),
                  _resident((1, d)),
                  pl.BlockSpec((d, tn), lambda i, j: (0, j))],
        out_specs=pl.BlockSpec((tm, tn), lambda i, j: (i, j)),
        out_shape=jax.ShapeDtypeStruct((t, n), F32),
        scratch_shapes=[pltpu.VMEM((tm, d), BF16)],
        compiler_params=_cparams("parallel", "arbitrary"),
        name="norm_proj",
    )(x, g, w)


def _ssd_prep_body(p_ref, bias_ref, alog_ref, dt_ref, cs_ref, *, q):
    raw = p_ref[:, 0:LANES]
    dt = _softplus(raw + bias_ref[...])
    dta = dt * (-jnp.exp(alog_ref[...]))
    dt_ref[...] = dt
    incl_f, _ = _tri_masks(q, True)
    incl_b, _ = _tri_masks(q, False)
    tril = incl_f.astype(F32)
    triu = incl_b.astype(F32)
    lane = lax.broadcasted_iota(jnp.int32, (q, LANES), 1)
    n_heads = SSD_N_GROUPS * SSD_HEADS_PER_GROUP
    for c in range(raw.shape[0] // q):
        blk = dta[c * q:(c + 1) * q, :]
        cs_ref[c * q:(c + 1) * q, :] = jnp.where(lane < n_heads, _dot_f32(tril, blk), _dot_f32(triu, blk))


def _ssd_prep(p, bias, alog):
    t = p.shape[0]
    tm = TOKEN_TILE
    blk = pl.BlockSpec((tm, LANES), lambda i: (i, 0))
    return pl.pallas_call(
        functools.partial(_ssd_prep_body, q=SSD_CHUNK),
        grid=(t // tm,),
        in_specs=[pl.BlockSpec((tm, 2 * LANES), lambda i: (i, (PROJ_PAD - 2 * LANES) // (2 * LANES))),
                  _resident((1, LANES)), _resident((1, LANES))],
        out_specs=[blk, blk],
        out_shape=[jax.ShapeDtypeStruct((t, LANES), F32)] * 2,
        compiler_params=_cparams("parallel"),
        name="ssd_prep",
    )(p, bias, alog)


def _ssd_core_body(xs_ref, bm_ref, cm_ref, wx_ref, wb_ref, wc_ref, bx_ref, bb_ref, bc_ref, dsk_ref,
                   dtr_ref, csr_ref, csc_ref, o_ref, xs_s, b_s, c_s, y_s, st_ref, *, q, n_chunks):
    hd, nh = SSD_HEAD_DIM, SSD_HEADS_PER_GROUP
    xs_s[...] = _dwconv_silu(xs_ref[...], wx_ref[...], bx_ref[...])
    b_s[...] = _dwconv_silu(bm_ref[...], wb_ref[...], bb_ref[...])
    c_s[...] = _dwconv_silu(cm_ref[...], wc_ref[...], bc_ref[...]).astype(BF16)
    st_ref[...] = jnp.zeros_like(st_ref)

    def chunk(c, forward):
        r0 = pl.multiple_of(c * q, q)
        incl, _ = _tri_masks(q, forward)
        cc = c_s[pl.ds(r0, q), :]
        bc = b_s[pl.ds(r0, q), :]
        cb = _dot_nt(cc, bc)
        bct = bc.T
        xc = xs_s[pl.ds(r0, q), :].astype(BF16)
        acols = csc_ref[pl.ds(r0, q), :]
        arows = csr_ref[c]
        dtrows = dtr_ref[c]
        outs = []
        for h in range(nh):
            i = h if forward else nh + h
            acol = acols[:, i:i + 1]
            arow = arows[i:i + 1, :]
            dtr = dtrows[i:i + 1, :]
            xh = xc[:, h * hd:(h + 1) * hd]
            lmat = jnp.where(incl, jnp.exp(acol - arow), 0.0) * cb * dtr
            tot = arow[:, q - 1:q] if forward else arow[:, 0:1]
            st = st_ref[i]
            y = _dot(lmat, xh) + _dot(cc, st) * jnp.exp(acol)
            st_ref[i] = st * jnp.exp(tot) + _dot(bct * (jnp.exp(tot - arow) * dtr), xh)
            outs.append(y)
        y = jnp.concatenate(outs, axis=1)
        if forward:
            y_s[pl.ds(r0, q), :] = y
        else:
            y_s[pl.ds(r0, q), :] += y

    def fwd(c, carry):
        chunk(c, True)
        return carry

    def bwd(c, carry):
        chunk(n_chunks - 1 - c, False)
        return carry

    lax.fori_loop(0, n_chunks, fwd, 0)
    lax.fori_loop(0, n_chunks, bwd, 0)
    o_ref[...] = y_s[...] + xs_s[...] * dsk_ref[...]


def _ssd_core(p3, conv_w, conv_b, dskip, dt_rows, cs_rows, cs_cols):
    b, l, _ = p3.shape
    q = SSD_CHUNK
    nc = l // q
    gw = SSD_HEADS_PER_GROUP * SSD_HEAD_DIM
    ns = SSD_D_STATE
    x0 = SSD_D_INNER // gw
    b0 = 2 * SSD_D_INNER // ns
    c0 = b0 + SSD_N_GROUPS
    wb0 = SSD_D_INNER // ns
    wc0 = wb0 + SSD_N_GROUPS
    rows_spec = pl.BlockSpec((None, None, nc, 8, q), lambda i, g: (i, g, 0, 0, 0))
    return pl.pallas_call(
        functools.partial(_ssd_core_body, q=q, n_chunks=nc),
        grid=(b, SSD_N_GROUPS),
        in_specs=[pl.BlockSpec((None, l, gw), lambda i, g: (i, 0, x0 + g)),
                  pl.BlockSpec((None, l, ns), lambda i, g: (i, 0, b0 + g)),
                  pl.BlockSpec((None, l, ns), lambda i, g: (i, 0, c0 + g)),
                  pl.BlockSpec((CONV_K, gw), lambda i, g: (0, g)),
                  pl.BlockSpec((CONV_K, ns), lambda i, g: (0, wb0 + g)),
                  pl.BlockSpec((CONV_K, ns), lambda i, g: (0, wc0 + g)),
                  pl.BlockSpec((1, gw), lambda i, g: (0, g)),
                  pl.BlockSpec((1, ns), lambda i, g: (0, wb0 + g)),
                  pl.BlockSpec((1, ns), lambda i, g: (0, wc0 + g)),
                  pl.BlockSpec((1, gw), lambda i, g: (0, g)),
                  rows_spec, rows_spec,
                  pl.BlockSpec((None, None, l, 8), lambda i, g: (i, g, 0, 0))],
        out_specs=pl.BlockSpec((None, l, gw), lambda i, g: (i, 0, g)),
        out_shape=jax.ShapeDtypeStruct((b, l, SSD_D_INNER), F32),
        scratch_shapes=[pltpu.VMEM((l, gw), F32), pltpu.VMEM((l, ns), F32), pltpu.VMEM((l, ns), BF16),
                        pltpu.VMEM((l, gw), F32), pltpu.VMEM((2 * SSD_HEADS_PER_GROUP, ns, SSD_HEAD_DIM), F32)],
        compiler_params=_cparams("parallel", "parallel"),
        name="ssd_core",
    )(p3, p3, p3, conv_w, conv_w, conv_w, conv_b, conv_b, conv_b, dskip, dt_rows, cs_rows, cs_cols)


def _ssd_out_body(y_ref, z_ref, nw_ref, w_ref, x_ref, o_ref):
    yz = y_ref[...] * _silu(z_ref[...])
    o_ref[...] = x_ref[...] + jnp.dot(_rms(yz, nw_ref[...]).astype(BF16), w_ref[...], preferred_element_type=F32)


def _ssd_out(y, p, norm_w, w_out, x):
    t, d = x.shape
    tm = TOKEN_TILE
    di = SSD_D_INNER
    return pl.pallas_call(
        _ssd_out_body,
        grid=(t // tm,),
        in_specs=[pl.BlockSpec((tm, di), lambda i: (i, 0)),
                  pl.BlockSpec((tm, di), lambda i: (i, 0)),
                  _resident((1, di)), _resident(w_out.shape),
                  pl.BlockSpec((tm, d), lambda i: (i, 0))],
        out_specs=pl.BlockSpec((tm, d), lambda i: (i, 0)),
        out_shape=jax.ShapeDtypeStruct((t, d), F32),
        compiler_params=_cparams("parallel"),
        name="ssd_out",
    )(y, p, norm_w, w_out, x)


def _group_rows_cols(a, b, l, nc, q):
    g = a.shape[2]
    cols = jnp.transpose(a, (0, 2, 1, 3))
    rows = jnp.transpose(a.reshape(b, nc, q, g, 8), (0, 3, 1, 4, 2))
    return rows, cols


def _ssd_mixer(x, b, l, g_norm, w_in, conv_w, conv_b, a_log, dt_bias, d_skip, norm_w, w_out):
    t = b * l
    q = SSD_CHUNK
    nc = l // q
    p = _proj(x, g_norm, w_in, 1280)
    dt, cs = _ssd_prep(p, dt_bias, a_log)
    ng, nh = SSD_N_GROUPS, SSD_HEADS_PER_GROUP

    def regroup(a):
        a = a[:, :2 * ng * nh].reshape(b, l, 2, ng, nh)
        return jnp.transpose(a, (0, 1, 3, 2, 4)).reshape(b, l, ng, 2 * nh)

    dt_rows, _ = _group_rows_cols(regroup(dt), b, l, nc, q)
    cs_rows, cs_cols = _group_rows_cols(regroup(cs), b, l, nc, q)
    y = _ssd_core(p.reshape(b, l, PROJ_PAD), conv_w, conv_b, d_skip, dt_rows, cs_rows, cs_cols)
    return _ssd_out(y.reshape(t, SSD_D_INNER), p, norm_w, w_out, x)


def _gdn_prep_body(p_ref, bias_ref, alog_ref, o_ref, *, cs):
    raw = p_ref[:, 0:LANES]
    beta = _sigmoid(raw)
    g = -jnp.exp(alog_ref[...]) * _softplus(raw + bias_ref[...])
    n = LANES
    li = lax.broadcasted_iota(jnp.int32, (n, n), 0)
    si = lax.broadcasted_iota(jnp.int32, (n, n), 1)
    sh = int(math.log2(cs))
    same = jnp.right_shift(li, sh) == jnp.right_shift(si, sh)
    tril = (same & (li >= si)).astype(F32)
    triu = (same & (li <= si)).astype(F32)
    lane = lax.broadcasted_iota(jnp.int32, (n, LANES), 1)
    hv = GDN_N_V_HEADS
    for c in range(raw.shape[0] // n):
        blk = g[c * n:(c + 1) * n, :]
        o_ref[c * n:(c + 1) * n, :] = jnp.where(
            lane < hv, beta[c * n:(c + 1) * n, :],
            jnp.where(lane < 2 * hv, _dot_f32(tril, blk), _dot_f32(triu, blk)))


def _gdn_prep(p, bias, alog):
    t = p.shape[0]
    tm = TOKEN_TILE
    return pl.pallas_call(
        functools.partial(_gdn_prep_body, cs=GDN_CHUNK),
        grid=(t // tm,),
        in_specs=[pl.BlockSpec((tm, 2 * LANES), lambda i: (i, (PROJ_PAD - 2 * LANES) // (2 * LANES))),
                  _resident((1, LANES)), _resident((1, LANES))],
        out_specs=pl.BlockSpec((tm, LANES), lambda i: (i, 0)),
        out_shape=jax.ShapeDtypeStruct((t, LANES), F32),
        compiler_params=_cparams("parallel"),
        name="gdn_prep",
    )(p, bias, alog)


def _gdn_core_body(q_ref, k_ref, v_ref, wq_ref, wk_ref, wv_ref, bq_ref, bk_ref, bv_ref, rows_ref, cols_ref,
                   o_ref, q_s, k_s, v_s, u_s, wq_s, qkk_s, yf_s, yb_s, st_ref, *, cs, n_chunks):
    hd = GDN_HEAD
    qn = _dwconv_silu(q_ref[...], wq_ref[...], bq_ref[...])
    qn = qn * lax.rsqrt(jnp.sum(qn * qn, axis=-1, keepdims=True) + 1e-6)
    q_s[...] = qn * (hd ** -0.5)
    kn = _dwconv_silu(k_ref[...], wk_ref[...], bk_ref[...])
    k_s[...] = kn * lax.rsqrt(jnp.sum(kn * kn, axis=-1, keepdims=True) + 1e-6)
    v_s[...] = _dwconv_silu(v_ref[...], wv_ref[...], bv_ref[...])
    st_ref[...] = jnp.zeros_like(st_ref)

    def factor(i, carry):
        units = []
        for j in range(FACTOR_CHUNKS):
            c = i * FACTOR_CHUNKS + j
            r0 = pl.multiple_of(c * cs, cs)
            qc = q_s[pl.ds(r0, cs), :]
            kc = k_s[pl.ds(r0, cs), :]
            vc = v_s[pl.ds(r0, cs), :]
            units.append((c, r0, qc, kc, vc, kc.T, cols_ref[pl.ds(r0, cs), :], rows_ref[c],
                          _dot_nt(kc, kc), _dot_nt(qc, kc)))
        nmats, rhss, dests = [], [], []
        for c, r0, qc, kc, vc, kt, cols, rows, kk, qk0 in units:
            c2 = pl.multiple_of(c * 2 * cs, 2 * cs)
            c3 = pl.multiple_of(c * 3 * cs, cs)
            for d, forward in enumerate((True, False)):
                incl, strict = _tri_masks(cs, forward)
                for e in range(2):
                    gi = 2 + 2 * d + e
                    si = 2 * d + e
                    bcol = cols[:, e:e + 1]
                    gcol = cols[:, gi:gi + 1]
                    grow = rows[gi:gi + 1, :]
                    gl = grow[:, cs - 1:cs] if forward else grow[:, 0:1]
                    dec = jnp.where(incl, jnp.exp(gcol - grow), 0.0)
                    eg = jnp.exp(gcol)
                    nmats.append(jnp.where(strict, -(bcol * kk * dec), 0.0))
                    rhss.append(jnp.concatenate([vc[:, e * hd:(e + 1) * hd] * bcol, kc * (bcol * eg)], axis=1))
                    wq_s[si, pl.ds(c2 + cs, cs), :] = (qc * eg).astype(BF16)
                    qkk_s[si, pl.ds(c3, cs), :] = jnp.where(incl, qk0 * dec, 0.0).astype(BF16)
                    qkk_s[si, pl.ds(c3 + cs, 2 * cs), :] = (kt * jnp.exp(gl - grow)).astype(BF16)
                    dests.append((si, r0, c2))
        for (si, r0, c2), sol in zip(dests, _tri_apply_many(nmats, rhss, cs)):
            u_s[si, pl.ds(r0, cs), :] = sol[:, :hd]
            wq_s[si, pl.ds(c2, cs), :] = sol[:, hd:].astype(BF16)
        return carry

    def scan(i, carry):
        units = []
        for d, forward in enumerate((True, False)):
            c = i if forward else n_chunks - 1 - i
            rows = rows_ref[c]
            for e in range(2):
                grow = rows[2 + 2 * d + e:3 + 2 * d + e, :]
                gl = grow[:, cs - 1:cs] if forward else grow[:, 0:1]
                units.append((2 * d + e, pl.multiple_of(c * cs, cs), pl.multiple_of(c * 2 * cs, 2 * cs),
                              pl.multiple_of(c * 3 * cs, cs), jnp.exp(gl)))
        states = [st_ref[si] for si, *_ in units]
        lv1 = [jnp.dot(wq_s[si, pl.ds(c2, 2 * cs), :], s.astype(BF16), preferred_element_type=F32)
               for (si, _, c2, _, _), s in zip(units, states)]
        vns = [(u_s[si, pl.ds(r0, cs), :] - l1[:cs, :]).astype(BF16) for (si, r0, *_), l1 in zip(units, lv1)]
        lv2 = [jnp.dot(qkk_s[si, pl.ds(c3, 3 * cs), :], vn, preferred_element_type=F32)
               for (si, _, _, c3, _), vn in zip(units, vns)]
        for (si, _, _, _, eg), s, l2 in zip(units, states, lv2):
            st_ref[si] = s * eg + l2[cs:, :]
        os_ = [l1[cs:, :] + l2[:cs, :] for l1, l2 in zip(lv1, lv2)]
        yf_s[pl.ds(units[0][1], cs), :] = jnp.concatenate(os_[0:2], axis=1)
        yb_s[pl.ds(units[2][1], cs), :] = jnp.concatenate(os_[2:4], axis=1)
        return carry

    lax.fori_loop(0, n_chunks // FACTOR_CHUNKS, factor, 0)
    lax.fori_loop(0, n_chunks, scan, 0)
    o_ref[...] = yf_s[...] + yb_s[...]


def _gdn_core(p3, conv_w, conv_b, rows, cols):
    b, l, _ = p3.shape
    cs = GDN_CHUNK
    nc = l // cs
    hd = GDN_HEAD
    vw = 2 * hd
    k0 = GDN_QK_DIM // hd
    v0 = 2 * GDN_QK_DIM // vw
    return pl.pallas_call(
        functools.partial(_gdn_core_body, cs=cs, n_chunks=nc),
        grid=(b, GDN_N_K_HEADS),
        in_specs=[pl.BlockSpec((None, l, hd), lambda i, j: (i, 0, j)),
                  pl.BlockSpec((None, l, hd), lambda i, j: (i, 0, k0 + j)),
                  pl.BlockSpec((None, l, vw), lambda i, j: (i, 0, v0 + j)),
                  pl.BlockSpec((CONV_K, hd), lambda i, j: (0, j)),
                  pl.BlockSpec((CONV_K, hd), lambda i, j: (0, k0 + j)),
                  pl.BlockSpec((CONV_K, vw), lambda i, j: (0, v0 + j)),
                  pl.BlockSpec((1, hd), lambda i, j: (0, j)),
                  pl.BlockSpec((1, hd), lambda i, j: (0, k0 + j)),
                  pl.BlockSpec((1, vw), lambda i, j: (0, v0 + j)),
                  pl.BlockSpec((None, None, nc, 8, cs), lambda i, j: (i, j, 0, 0, 0)),
                  pl.BlockSpec((None, None, l, 8), lambda i, j: (i, j, 0, 0))],
        out_specs=pl.BlockSpec((None, l, vw), lambda i, j: (i, 0, j)),
        out_shape=jax.ShapeDtypeStruct((b, l, GDN_V_DIM), F32),
        scratch_shapes=[pltpu.VMEM((l, hd), F32), pltpu.VMEM((l, hd), F32), pltpu.VMEM((l, vw), F32),
                        pltpu.VMEM((4, l, hd), F32),
                        pltpu.VMEM((4, 2 * l, hd), BF16),
                        pltpu.VMEM((4, 3 * l, cs), BF16),
                        pltpu.VMEM((l, vw), F32), pltpu.VMEM((l, vw), F32), pltpu.VMEM((4, hd, hd), F32)],
        compiler_params=_cparams("parallel", "parallel"),
        name="gdn_core",
    )(p3, p3, p3, conv_w, conv_w, conv_w, conv_b, conv_b, conv_b, rows, cols)


def _gdn_out_body(o_ref, z_ref, nw_ref, w_ref, x_ref, out_ref, h_s):
    hd = GDN_HEAD
    nw = nw_ref[...]
    for e in range(GDN_N_V_HEADS):
        oe = o_ref[:, e * hd:(e + 1) * hd]
        h_s[:, e * hd:(e + 1) * hd] = (_rms(oe, nw) * _silu(z_ref[:, e * hd:(e + 1) * hd])).astype(BF16)
    out_ref[...] = x_ref[...] + jnp.dot(h_s[...], w_ref[...], preferred_element_type=F32)


def _gdn_out(o, p, norm_w, w_out, x):
    t, d = x.shape
    tm = TOKEN_TILE
    dv = GDN_V_DIM
    return pl.pallas_call(
        _gdn_out_body,
        grid=(t // tm,),
        in_specs=[pl.BlockSpec((tm, dv), lambda i: (i, 0)),
                  pl.BlockSpec((tm, dv), lambda i: (i, 2 * GDN_QK_DIM // dv + 1)),
                  _resident((1, GDN_HEAD)), _resident(w_out.shape),
                  pl.BlockSpec((tm, d), lambda i: (i, 0))],
        out_specs=pl.BlockSpec((tm, d), lambda i: (i, 0)),
        out_shape=jax.ShapeDtypeStruct((t, d), F32),
        scratch_shapes=[pltpu.VMEM((tm, dv), BF16)],
        compiler_params=_cparams("parallel"),
        name="gdn_out",
    )(o, p, norm_w, w_out, x)


def _gdn_mixer(x, b, l, g_norm, w_in, conv_w, conv_b, a_log, dt_bias, norm_w, w_out):
    t = b * l
    cs = GDN_CHUNK
    nc = l // cs
    p = _proj(x, g_norm, w_in, 1280)
    gt = _gdn_prep(p, dt_bias, a_log).reshape(b, l, LANES)
    hk, hv = GDN_N_K_HEADS, GDN_N_V_HEADS
    parts = [gt[..., i * hv:(i + 1) * hv].reshape(b, l, hk, 2) for i in range(3)]
    a = jnp.concatenate(parts + [jnp.zeros((b, l, hk, 2), F32)], axis=-1)
    rows, cols = _group_rows_cols(a, b, l, nc, cs)
    o = _gdn_core(p.reshape(b, l, PROJ_PAD), conv_w, conv_b, rows, cols)
    return _gdn_out(o.reshape(t, GDN_V_DIM), p, norm_w, w_out, x)


def _att_core_body(q_ref, k_ref, v_ref, cos_ref, sin_ref, sink_ref, o_ref, q_s, k_s, v_s, band_s,
                   *, win, n_blocks, seq):
    dh = ATT_HEAD_DIM
    cos = cos_ref[...]
    sin = sin_ref[...]

    def rope(x):
        w = x.shape[1]
        lane = lax.broadcasted_iota(jnp.int32, x.shape, 1)
        first = jnp.bitwise_and(lane, dh - 1) < dh // 2
        partner = jnp.where(first, pltpu.roll(x, w - dh // 2, axis=1), pltpu.roll(x, dh // 2, axis=1))
        reps = w // LANES
        c = jnp.concatenate([cos] * reps, axis=1) if reps > 1 else cos
        s = jnp.concatenate([sin] * reps, axis=1) if reps > 1 else sin
        return x * c + partner * s

    n_heads = q_ref.shape[1] // dh
    rep = ATT_N_HEADS // ATT_N_KV_HEADS
    groups = range(n_heads // rep)
    qr = rope(q_ref[...]) * (dh ** -0.5)
    for h in range(n_heads):
        q_s[h] = qr[:, h * dh:(h + 1) * dh].astype(BF16)
    kr = rope(k_ref[...])
    vv = v_ref[...]
    ones = jnp.ones((seq, dh), F32)
    for g in groups:
        for ref in (k_s, v_s):
            ref[g, pl.ds(0, win), :] = jnp.zeros((win, ref.shape[2]), BF16)
            ref[g, pl.ds(win + seq, win), :] = jnp.zeros((win, ref.shape[2]), BF16)
        k_s[g, pl.ds(win, seq), :] = kr[:, g * dh:(g + 1) * dh].astype(BF16)
        v_s[g, pl.ds(win, seq), :] = jnp.concatenate([vv[:, g * dh:(g + 1) * dh], ones], axis=1).astype(BF16)
    qi = lax.broadcasted_iota(jnp.int32, (win, 3 * win), 0)
    ki = lax.broadcasted_iota(jnp.int32, (win, 3 * win), 1)
    band_s[...] = jnp.where(jnp.abs(ki - win - qi) <= win, 0.0, -jnp.inf)
    sinks = sink_ref[...]

    def block(n, carry):
        r0 = pl.multiple_of(n * win, win)
        kpos = r0 - win + lax.broadcasted_iota(jnp.int32, (1, 3 * win), 1)
        bias = band_s[...] + jnp.where((kpos >= 0) & (kpos < seq), 0.0, -jnp.inf)
        scores = []
        for g in groups:
            qg = jnp.concatenate([q_s[g * rep + r, pl.ds(r0, win), :] for r in range(rep)], axis=0)
            scores.append(_dot_nt(qg, k_s[g, pl.ds(r0, 3 * win), :]))
        exps, maxes = [], []
        for g in groups:
            es = []
            for r in range(rep):
                s = scores[g][r * win:(r + 1) * win, :] + bias
                sink = sinks[:, g * rep + r:g * rep + r + 1]
                m = jnp.maximum(jnp.max(s, axis=-1, keepdims=True), sink)
                es.append(jnp.exp(s - m).astype(BF16))
                maxes.append(m)
            exps.append(jnp.concatenate(es, axis=0))
        nds = [jnp.dot(exps[g], v_s[g, pl.ds(r0, 3 * win), :], preferred_element_type=F32) for g in groups]
        outs = []
        for g in groups:
            for r in range(rep):
                h = g * rep + r
                nd = nds[g][r * win:(r + 1) * win, :]
                outs.append(nd[:, :dh] / (nd[:, dh:] + jnp.exp(sinks[:, h:h + 1] - maxes[h])))
        o_ref[pl.ds(r0, win), :] = jnp.concatenate(outs, axis=1).astype(o_ref.dtype)
        return carry

    lax.fori_loop(0, n_blocks, block, 0)


def _att_core(p3, cos2, sin2, sinks3):
    b, l, _ = p3.shape
    win = ATT_WINDOW
    dh = ATT_HEAD_DIM
    steps = ATT_N_KV_HEADS * dh // LANES
    qw = ATT_N_HEADS * dh // steps
    k0 = ATT_N_HEADS * dh // LANES
    v0 = k0 + steps
    return pl.pallas_call(
        functools.partial(_att_core_body, win=win, n_blocks=l // win, seq=l),
        grid=(b, steps),
        in_specs=[pl.BlockSpec((None, l, qw), lambda i, j: (i, 0, j)),
                  pl.BlockSpec((None, l, LANES), lambda i, j: (i, 0, k0 + j)),
                  pl.BlockSpec((None, l, LANES), lambda i, j: (i, 0, v0 + j)),
                  _resident((l, LANES)), _resident((l, LANES)),
                  pl.BlockSpec((None, 1, qw // dh), lambda i, j: (j, 0, 0))],
        out_specs=pl.BlockSpec((None, l, qw), lambda i, j: (i, 0, j)),
        out_shape=jax.ShapeDtypeStruct((b, l, ATT_N_HEADS * dh), BF16),
        scratch_shapes=[pltpu.VMEM((qw // dh, l, dh), BF16),
                        pltpu.VMEM((LANES // dh, l + 2 * win, dh), BF16),
                        pltpu.VMEM((LANES // dh, l + 2 * win, 2 * dh), BF16),
                        pltpu.VMEM((win, 3 * win), F32)],
        compiler_params=_cparams("parallel", "parallel"),
        name="att_core",
    )(p3, p3, p3, cos2, sin2, sinks3)


def _res_proj_body(a_ref, w_ref, x_ref, o_ref):
    o_ref[...] = x_ref[...] + jnp.dot(a_ref[...], w_ref[...], preferred_element_type=F32)


def _res_proj(a, w, x):
    t, d = x.shape
    k = a.shape[1]
    tm = TOKEN_TILE
    return pl.pallas_call(
        _res_proj_body,
        grid=(t // tm,),
        in_specs=[pl.BlockSpec((tm, k), lambda i: (i, 0)), _resident(w.shape),
                  pl.BlockSpec((tm, d), lambda i: (i, 0))],
        out_specs=pl.BlockSpec((tm, d), lambda i: (i, 0)),
        out_shape=jax.ShapeDtypeStruct((t, d), F32),
        compiler_params=_cparams("parallel"),
        name="res_proj",
    )(a, w, x)


def _att_mixer(x, b, l, g_norm, w_qkv, cos2, sin2, sinks3, w_out):
    t = b * l
    p = _proj(x, g_norm, w_qkv, w_qkv.shape[1] // 2)
    o = _att_core(p.reshape(b, l, -1), cos2, sin2, sinks3)
    return _res_proj(o.reshape(t, -1), w_out, x)


def _rwkv_prep_body(x_ref, xp_ref, xn_ref, g_ref, mu_ref, wr_ref, wk_ref, wv_ref, w1_ref, w2_ref, w0_ref,
                    a0_ref, a1_ref, a2_ref, g1_ref, g2_ref, kk_ref, ka_ref,
                    r_o, k_o, v_o, kk_o, a_o, lw0_o, lw1_o, gate_o, *, n_tiles):
    i = pl.program_id(1)
    g = g_ref[...]
    u = _rms(x_ref[...], g)
    tl = u.shape[0]
    prev_row = jnp.where(i > 0, _rms(xp_ref[...], g)[7:8, :], 0.0)
    next_row = jnp.where(i < n_tiles - 1, _rms(xn_ref[...], g)[0:1, :], 0.0)
    row = lax.broadcasted_iota(jnp.int32, u.shape, 0)
    u_prev = jnp.where(row == 0, prev_row, pltpu.roll(u, 1, axis=0))
    u_next = jnp.where(row == tl - 1, next_row, pltpu.roll(u, tl - 1, axis=0))
    xx = 0.5 * (u_prev + u_next) - u
    mu = mu_ref[...]

    def mix(s):
        return (u + xx * mu[s:s + 1, :]).astype(BF16)

    r_o[...] = jnp.dot(mix(0), wr_ref[...], preferred_element_type=F32)
    k = jnp.dot(mix(1), wk_ref[...], preferred_element_type=F32)
    v_o[...] = jnp.dot(mix(2), wv_ref[...], preferred_element_type=F32)
    wl = jnp.tanh(jnp.dot(mix(3), w1_ref[...], preferred_element_type=F32))
    lora = w2_ref.shape[1]
    w0 = w0_ref[...]
    for d, out in enumerate((lw0_o, lw1_o)):
        z = w0[d:d + 1, :] + _dot(wl[:, d * lora:(d + 1) * lora], w2_ref[d])
        out[...] = -jnp.exp(-_softplus(-z) - 0.5)
    a = _sigmoid(a0_ref[...] + _dot(jnp.dot(mix(4), a1_ref[...], preferred_element_type=F32), a2_ref[...]))
    gate_o[...] = _dot(_sigmoid(jnp.dot(mix(5), g1_ref[...], preferred_element_type=F32)), g2_ref[...])
    a_o[...] = a
    kk_o[...] = k * kk_ref[...]
    k_o[...] = k * (1.0 + (a - 1.0) * ka_ref[...])


def _rwkv_prep(x3, g, mu, wr, wk, wv, w1, w2, w0, a0, a1, a2, g1, g2, k_k, k_a):
    b, l, d = x3.shape
    tl = 256
    nt = l // tl
    x4 = x3.reshape(b, l // 8, 8, d)
    tile = pl.BlockSpec((None, tl, d), lambda i, j: (i, j, 0))
    consts = [g, mu, wr, wk, wv, w1, w2, w0, a0, a1, a2, g1, g2, k_k, k_a]
    return pl.pallas_call(
        functools.partial(_rwkv_prep_body, n_tiles=nt),
        grid=(b, nt),
        in_specs=[tile,
                  pl.BlockSpec((None, None, 8, d), lambda i, j: (i, jnp.maximum(j * (tl // 8) - 1, 0), 0, 0)),
                  pl.BlockSpec((None, None, 8, d),
                               lambda i, j: (i, jnp.minimum((j + 1) * (tl // 8), l // 8 - 1), 0, 0))]
                 + [_resident(c.shape) for c in consts],
        out_specs=[tile] * 8,
        out_shape=[jax.ShapeDtypeStruct((b, l, d), F32)] * 8,
        compiler_params=_cparams("parallel", "parallel"),
        name="rwkv_prep",
    )(x3, x4, x4, *consts)


def _rwkv_core_body(r_ref, k_ref, v_ref, kk_ref, a_ref, lw0_ref, lw1_ref, rk_ref, lnw_ref, lnb_ref, o_ref,
                    kk_s, kb_s, wr_s, u0_s, lv2_s, pe_s, yf_s, yb_s, st_ref, *, cs, n_chunks):
    n = RWKV_HEAD
    kk = kk_ref[...]
    kk = kk * lax.rsqrt(_seg_sum2(kk * kk) + 1e-6)
    kk_s[...] = kk
    kb_s[...] = kk * a_ref[...]
    st_ref[...] = jnp.zeros_like(st_ref)

    def factor(i, carry):
        span = FACTOR_CHUNKS * cs
        base = pl.multiple_of(i * span, span)
        rows = pl.ds(base, span)
        rc2, kc2, vc2, kkc2, kbc2 = r_ref[rows, :], k_ref[rows, :], v_ref[rows, :], kk_s[rows, :], kb_s[rows, :]
        li = lax.broadcasted_iota(jnp.int32, (span, span), 0)
        si = lax.broadcasted_iota(jnp.int32, (span, span), 1)
        sh = int(math.log2(cs))
        same = jnp.right_shift(li, sh) == jnp.right_shift(si, sh)
        lws, cums = [], []
        for forward, lw_ref in ((True, lw0_ref), (False, lw1_ref)):
            lws.append(lw_ref[rows, :])
            order = (li >= si) if forward else (li <= si)
            cums.append(_dot_f32((same & order).astype(F32), lws[-1]))
        li2 = lax.broadcasted_iota(jnp.int32, (cs, 2 * cs), 0)
        si2 = jnp.bitwise_and(lax.broadcasted_iota(jnp.int32, (cs, 2 * cs), 1), cs - 1)
        units, lhss, rhss = [], [], []
        for j in range(FACTOR_CHUNKS):
            rs = slice(j * cs, (j + 1) * cs)
            c = i * FACTOR_CHUNKS + j
            r0 = pl.multiple_of(c * cs, cs)
            for d, forward in enumerate((True, False)):
                lw, cum = lws[d][rs, :], cums[d][rs, :]
                tot = cum[cs - 1:cs, :] if forward else cum[0:1, :]
                e_neg = jnp.exp(-cum)
                e_end = jnp.exp(tot - cum)
                a_hat = -kkc2[rs, :] * jnp.exp(cum - lw)
                r_hat = rc2[rs, :] * jnp.exp(cum)
                b_hat = kbc2[rs, :] * e_neg
                k_hat = kc2[rs, :] * e_neg
                c2 = pl.multiple_of(c * 2 * cs, 2 * cs)
                c4 = pl.multiple_of(c * 4 * cs, 4 * cs)
                wr_s[d, pl.ds(c2 + cs, cs), :] = r_hat.astype(BF16)
                lv2_s[d, pl.ds(c4 + 2 * cs, 2 * cs), :] = jnp.concatenate(
                    [(kbc2[rs, :] * e_end).T, (kc2[rs, :] * e_end).T], axis=1).astype(BF16)
                p_col = jnp.broadcast_to(jnp.exp(tot), (8, LANES)).T[:, 0:1]
                pe_s[d, pl.ds(c2, 2 * cs), :] = jnp.broadcast_to(p_col, (LANES, LANES))
                for e in range(2):
                    sl = slice(e * n, (e + 1) * n)
                    lhss.append(jnp.concatenate([a_hat[:, sl], r_hat[:, sl]], axis=0))
                    rhss.append(jnp.concatenate([b_hat[:, sl], k_hat[:, sl]], axis=0))
                    units.append((d, e, r0, forward, a_hat[:, sl], vc2[rs, sl]))
        grams = [_dot_nt(lh, rh) for lh, rh in zip(lhss, rhss)]
        n_abs, a_aks = [], []
        for (d, e, r0, forward, _, _), gram in zip(units, grams):
            incl, strict = _tri_masks(cs, forward)
            incl2 = (li2 >= si2) if forward else (li2 <= si2)
            n_abs.append(jnp.where(strict, gram[:cs, :cs], 0.0))
            a_aks.append(jnp.where(strict, gram[:cs, cs:], 0.0))
            lv2_s[d, pl.ds(pl.multiple_of(r0 * 4, 4 * cs) + e * cs, cs), :] = jnp.where(
                incl2, gram[cs:, :], 0.0).astype(BF16)
        avs = [_dot(a_ak, ve) for a_ak, (*_, ve) in zip(a_aks, units)]
        xs = [jnp.concatenate([ah, av], axis=1) for (*_, ah, _), av in zip(units, avs)]
        sols = _tri_apply_many(n_abs, xs, cs)
        for u in range(0, len(units), 2):
            d, _, r0 = units[u][:3]
            wr_s[d, pl.ds(pl.multiple_of(r0 * 2, 2 * cs), cs), :] = jnp.concatenate(
                [sols[u][:, :n], sols[u + 1][:, :n]], axis=1).astype(BF16)
            u0_s[d, pl.ds(r0, cs), :] = jnp.concatenate([sols[u][:, n:], sols[u + 1][:, n:]], axis=1)
        return carry

    lane = lax.broadcasted_iota(jnp.int32, (LANES, LANES), 1)
    head0 = lax.broadcasted_iota(jnp.int32, (cs, LANES), 1) < n
    same_head = (lax.broadcasted_iota(jnp.int32, (LANES, LANES), 0) < n) == (lane < n)

    def scan(i, carry):
        cidx = [i, n_chunks - 1 - i]
        r0s = [pl.multiple_of(c * cs, cs) for c in cidx]
        c2s = [pl.multiple_of(c * 2 * cs, 2 * cs) for c in cidx]
        c4s = [pl.multiple_of(c * 4 * cs, 4 * cs) for c in cidx]
        states = [st_ref[d] for d in range(2)]
        lv1 = [jnp.dot(wr_s[d, pl.ds(c2s[d], 2 * cs), :], states[d].astype(BF16), preferred_element_type=F32)
               for d in range(2)]
        uvs = [jnp.concatenate([u0_s[d, pl.ds(r0s[d], cs), :] + lv1[d][:cs, :], v_ref[pl.ds(r0s[d], cs), :]],
                               axis=0).astype(BF16) for d in range(2)]
        lv2 = [jnp.dot(lv2_s[d, pl.ds(c4s[d], 4 * cs), :], uvs[d], preferred_element_type=F32)
               for d in range(2)]
        for d, y_s in enumerate((yf_s, yb_s)):
            y_s[pl.ds(r0s[d], cs), :] = lv1[d][cs:, :] + jnp.where(head0, lv2[d][:cs, :], lv2[d][cs:2 * cs, :])
            st_ref[d] = states[d] * pe_s[d, pl.ds(c2s[d], 2 * cs), :] + jnp.where(same_head, lv2[d][2 * cs:, :], 0.0)
        return carry

    lax.fori_loop(0, n_chunks // FACTOR_CHUNKS, factor, 0)
    lax.fori_loop(0, n_chunks, scan, 0)
    o = yf_s[...] + yb_s[...]
    mean = _seg_sum2(o) * (1.0 / n)
    cen = o - mean
    var = _seg_sum2(cen * cen) * (1.0 / n)
    y = cen * lax.rsqrt(var + RWKV_LN_EPS) * lnw_ref[...] + lnb_ref[...]
    bonus = _seg_sum2(r_ref[...] * k_ref[...] * rk_ref[...]) * v_ref[...]
    o_ref[...] = y + bonus


def _rwkv_core(r, k, v, kk, a, lw0, lw1, r_k, lnw, lnb):
    b, l, d = r.shape
    cs = RWKV_CHUNK
    pair = pl.BlockSpec((None, l, LANES), lambda i, j: (i, 0, j))
    vec = pl.BlockSpec((1, LANES), lambda i, j: (0, j))
    return pl.pallas_call(
        functools.partial(_rwkv_core_body, cs=cs, n_chunks=l // cs),
        grid=(b, d // LANES),
        in_specs=[pair] * 7 + [vec] * 3,
        out_specs=pair,
        out_shape=jax.ShapeDtypeStruct((b, l, d), F32),
        scratch_shapes=[pltpu.VMEM((l, LANES), F32), pltpu.VMEM((l, LANES), F32),
                        pltpu.VMEM((2, 2 * l, LANES), BF16),
                        pltpu.VMEM((2, l, LANES), F32),
                        pltpu.VMEM((2, 4 * l, LANES), BF16),
                        pltpu.VMEM((2, 2 * l, LANES), F32),
                        pltpu.VMEM((l, LANES), F32), pltpu.VMEM((l, LANES), F32),
                        pltpu.VMEM((2, LANES, LANES), F32)],
        compiler_params=_cparams("parallel", "parallel"),
        name="rwkv_core",
    )(r, k, v, kk, a, lw0, lw1, r_k, lnw, lnb)


def _gate_proj_body(y_ref, gate_ref, w_ref, x_ref, o_ref):
    o_ref[...] = x_ref[...] + jnp.dot((y_ref[...] * gate_ref[...]).astype(BF16), w_ref[...],
                                      preferred_element_type=F32)


def _gate_proj(y, gate, w, x):
    t, d = x.shape
    tm = TOKEN_TILE
    tile = pl.BlockSpec((tm, d), lambda i: (i, 0))
    return pl.pallas_call(
        _gate_proj_body,
        grid=(t // tm,),
        in_specs=[tile, tile, _resident(w.shape), tile],
        out_specs=tile,
        out_shape=jax.ShapeDtypeStruct((t, d), F32),
        compiler_params=_cparams("parallel"),
        name="rwkv_out",
    )(y, gate, w, x)


def _rwkv_mixer(x, b, l, g_norm, mu, wr, wk, wv, w1, w2, w0, a0, a1, a2, g1, g2, k_k, k_a, r_k, lnw, lnb, w_out):
    t, d = x.shape
    r, k, v, kk, a, lw0, lw1, gate = _rwkv_prep(x.reshape(b, l, d), g_norm, mu, wr, wk, wv, w1, w2, w0,
                                                a0, a1, a2, g1, g2, k_k, k_a)
    y = _rwkv_core(r, k, v, kk, a, lw0, lw1, r_k, lnw, lnb)
    return _gate_proj(y.reshape(t, d), gate.reshape(t, d), w_out, x)


def _row(v):
    return v.reshape(1, -1).astype(F32)


def _pad_cols(w, n):
    return jnp.pad(w, ((0, 0), (0, n - w.shape[1])))


def _pad_row(v, n):
    v = v.reshape(1, -1).astype(F32)
    return jnp.pad(v, ((0, 0), (0, n - v.shape[1])))


def _ffn_weights(w_gu, w_down):
    d, f2 = w_gu.shape
    f = f2 // 2
    nck = f // FFN_CHUNK
    wg3 = jnp.transpose(w_gu[:, :f].reshape(d, nck, FFN_CHUNK), (1, 0, 2)).astype(BF16)
    wu3 = jnp.transpose(w_gu[:, f:].reshape(d, nck, FFN_CHUNK), (1, 0, 2)).astype(BF16)
    wd3 = w_down.reshape(nck, FFN_CHUNK, d).astype(BF16)
    return wg3, wu3, wd3


def _rope_tables(l):
    half = ATT_HEAD_DIM // 2
    inv_freq = ROPE_THETA ** (-jnp.arange(half, dtype=F32) / half)
    ang = jnp.arange(l).astype(F32)[:, None] * inv_freq[None, :]
    cos, sin = jnp.cos(ang), jnp.sin(ang)
    reps = LANES // ATT_HEAD_DIM
    cos2 = jnp.tile(jnp.concatenate([cos, cos], axis=1), (1, reps))
    sin2 = jnp.tile(jnp.concatenate([-sin, sin], axis=1), (1, reps))
    return cos2, sin2


def kernel(x_prompt, x_sample, ffn1_norm, ffn1_w_gu, ffn1_w_down, mix_norm, ffn2_norm, ffn2_w_gu, ffn2_w_down, ssd_w_in, ssd_conv_w, ssd_conv_b, ssd_a_log, ssd_dt_bias, ssd_d, ssd_norm, ssd_w_out, gdn_w_in, gdn_conv_w, gdn_conv_b, gdn_a_log, gdn_dt_bias, gdn_norm, gdn_w_out, att_w_qkv, att_sinks, att_w_out, rwkv_x_mu, rwkv_w_rkv, rwkv_w0, rwkv_w1, rwkv_w2, rwkv_a0, rwkv_a1, rwkv_a2, rwkv_g1, rwkv_g2, rwkv_k_k, rwkv_k_a, rwkv_r_k, rwkv_lnx_w, rwkv_lnx_b, rwkv_w_out, final_norm):
    depth = ffn1_norm.shape[0]
    ffn1 = [_ffn_weights(ffn1_w_gu[i], ffn1_w_down[i]) for i in range(depth)]
    ffn2 = [_ffn_weights(ffn2_w_gu[i], ffn2_w_down[i]) for i in range(depth)]
    fin_g = _row(final_norm)
    hv = GDN_N_V_HEADS
    att_steps = ATT_N_KV_HEADS * ATT_HEAD_DIM // LANES

    def mixer(h, b, l, i):
        m, j = i % N_MIXERS, i // N_MIXERS
        g = _row(mix_norm[i])
        if m == 0:
            return _ssd_mixer(h, b, l, g, _pad_cols(ssd_w_in[j], PROJ_PAD).astype(BF16), ssd_conv_w[j],
                              _row(ssd_conv_b[j]), _pad_row(ssd_a_log[j], LANES), _pad_row(ssd_dt_bias[j], LANES),
                              _row(jnp.repeat(ssd_d[j], SSD_HEAD_DIM)), _row(ssd_norm[j]), ssd_w_out[j].astype(BF16))
        if m == 1:
            lead = jnp.zeros((1, hv), F32)
            alog = jnp.pad(jnp.concatenate([lead, _row(gdn_a_log[j])], axis=1), ((0, 0), (0, LANES - 3 * hv)))
            bias = jnp.pad(jnp.concatenate([lead, _row(gdn_dt_bias[j])], axis=1), ((0, 0), (0, LANES - 3 * hv)))
            return _gdn_mixer(h, b, l, g, _pad_cols(gdn_w_in[j], PROJ_PAD).astype(BF16), gdn_conv_w[j],
                              _row(gdn_conv_b[j]), alog, bias, _row(gdn_norm[j]), gdn_w_out[j].astype(BF16))
        if m == 2:
            cos2, sin2 = _rope_tables(l)
            return _att_mixer(h, b, l, g, att_w_qkv[j].astype(BF16), cos2, sin2,
                              att_sinks[j].astype(F32).reshape(att_steps, 1, -1), att_w_out[j].astype(BF16))
        w1 = jnp.concatenate([rwkv_w1[j, 0], rwkv_w1[j, 1]], axis=1).astype(BF16)
        return _rwkv_mixer(h, b, l, g, rwkv_x_mu[j], rwkv_w_rkv[j, 0].astype(BF16), rwkv_w_rkv[j, 1].astype(BF16),
                           rwkv_w_rkv[j, 2].astype(BF16), w1, rwkv_w2[j].astype(BF16), rwkv_w0[j],
                           _row(rwkv_a0[j]), rwkv_a1[j].astype(BF16), rwkv_a2[j].astype(BF16),
                           rwkv_g1[j].astype(BF16), rwkv_g2[j].astype(BF16), _row(rwkv_k_k[j]), _row(rwkv_k_a[j]),
                           _row(rwkv_r_k[j]), _row(rwkv_lnx_w[j]), _row(rwkv_lnx_b[j]), rwkv_w_out[j].astype(BF16))

    def trunk(x3):
        b, l, d = x3.shape
        x = x3.reshape(b * l, d)
        for i in range(depth):
            x = _ffn(x, _row(ffn1_norm[i]), *ffn1[i], fin_g, False)
            x = mixer(x, b, l, i)
            x = _ffn(x, _row(ffn2_norm[i]), *ffn2[i], fin_g, i == depth - 1)
        return x.reshape(b, l, d)

    return (trunk(x_prompt), trunk(x_sample))
```

```python
import functools
import math

import jax
import jax.numpy as jnp
from jax import lax
from jax.experimental import pallas as pl
from jax.experimental.pallas import tpu as pltpu

F32 = jnp.float32
BF16 = jnp.bfloat16

V7X_VMEM_LIMIT_BYTES = 56 * 1024 * 1024
LANES = 128

RMS_EPS = 1e-6
CONV_K = 5
N_MIXERS = 4

SSD_HEAD_DIM = 64
SSD_N_GROUPS = 8
SSD_HEADS_PER_GROUP = 4
SSD_D_STATE = 128
SSD_CHUNK = 128
SSD_D_INNER = 2048

GDN_N_K_HEADS = 8
GDN_N_V_HEADS = 16
GDN_HEAD = 128
GDN_CHUNK = 64
GDN_QK_DIM = 1024
GDN_V_DIM = 2048

ATT_N_HEADS = 16
ATT_N_KV_HEADS = 4
ATT_HEAD_DIM = 64
ATT_WINDOW = 128
ROPE_THETA = 10000.0

RWKV_HEAD = 64
RWKV_CHUNK = 64
RWKV_LN_EPS = 64e-5

PROJ_AUX = 256
PROJ_PAD = 6144 + PROJ_AUX
PROJ_CHUNK = 512

TOKEN_TILE = 512
FFN_CHUNK = 256
FACTOR_CHUNKS = 8


def _cparams(*sem):
    return pltpu.CompilerParams(dimension_semantics=sem, vmem_limit_bytes=V7X_VMEM_LIMIT_BYTES)


def _resident(shape):
    nd = len(shape)
    return pl.BlockSpec(shape, lambda *_: (0,) * nd, pipeline_mode=pl.Buffered(1))


def _sigmoid(x):
    return 1.0 / (1.0 + jnp.exp(-x))


def _silu(x):
    return x * _sigmoid(x)


def _softplus(x):
    return jnp.maximum(x, 0.0) + jnp.log1p(jnp.exp(-jnp.abs(x)))


def _rms(x, g):
    return x * lax.rsqrt(jnp.mean(x * x, axis=-1, keepdims=True) + RMS_EPS) * g


def _dot(a, b):
    return jnp.dot(a.astype(BF16), b.astype(BF16), preferred_element_type=F32)


def _dot_nt(a, b):
    return lax.dot_general(a.astype(BF16), b.astype(BF16), (((1,), (1,)), ((), ())), preferred_element_type=F32)


def _dot_f32(a, b):
    return jnp.dot(a, b, precision=lax.Precision.HIGHEST, preferred_element_type=F32)


def _tri_apply_many(ns, xs, size):
    steps = int(math.log2(size))
    ms = list(ns)
    xs = list(xs)
    for s in range(steps):
        prods = [_dot(m, x) for m, x in zip(ms, xs)]
        if s + 1 < steps:
            ms = [_dot(m, m) for m in ms]
        xs = [x + p for x, p in zip(xs, prods)]
    return xs


def _dwconv_silu(x, w, b):
    n = x.shape[0]
    half = CONV_K // 2

    def taps(xx, masked):
        m = xx.shape[0]
        acc = xx * w[half:half + 1, :] + b
        for k in range(-half, half + 1):
            if k == 0:
                continue
            xr = pltpu.roll(xx, (-k) % m, axis=0)
            if masked:
                row = lax.broadcasted_iota(jnp.int32, xx.shape, 0)
                xr = jnp.where((row + k >= 0) & (row + k < m), xr, 0.0)
            acc = acc + xr * w[k + half:k + half + 1, :]
        return acc

    edge = 8
    head = taps(x[:2 * edge, :], True)[:edge, :]
    tail = taps(x[n - 2 * edge:, :], True)[edge:, :]
    acc = jnp.concatenate([head, taps(x, False)[edge:n - edge, :], tail], axis=0)
    return _silu(acc)


def _seg_sum2(x):
    lane = lax.broadcasted_iota(jnp.int32, x.shape, 1)
    lo = lane < 64
    s_lo = jnp.sum(jnp.where(lo, x, 0.0), axis=1, keepdims=True)
    s_hi = jnp.sum(jnp.where(lo, 0.0, x), axis=1, keepdims=True)
    return jnp.where(lo, s_lo, s_hi)


def _tri_masks(n, forward):
    li = lax.broadcasted_iota(jnp.int32, (n, n), 0)
    si = lax.broadcasted_iota(jnp.int32, (n, n), 1)
    if forward:
        return li >= si, li > si
    return li <= si, li < si


def _ffn_body(x_ref, g_ref, wg_ref, wu_ref, wd_ref, fg_ref, o_ref, xn_ref, acc_ref, *, n_chunks, final):
    x = x_ref[...]
    xn_ref[...] = _rms(x, g_ref[...]).astype(BF16)
    acc_ref[...] = jnp.zeros_like(acc_ref)

    def chunk(c, carry):
        xb = xn_ref[...]
        gate = jnp.dot(xb, wg_ref[c], preferred_element_type=F32)
        up = jnp.dot(xb, wu_ref[c], preferred_element_type=F32)
        h = (_silu(gate) * up).astype(BF16)
        acc_ref[...] += jnp.dot(h, wd_ref[c], preferred_element_type=F32)
        return carry

    lax.fori_loop(0, n_chunks, chunk, 0)
    y = x + 0.5 * acc_ref[...]
    if final:
        y = _rms(y, fg_ref[...])
    o_ref[...] = y


def _ffn(x, g, wg3, wu3, wd3, final_g, final):
    t, d = x.shape
    tm = TOKEN_TILE
    body = functools.partial(_ffn_body, n_chunks=wg3.shape[0], final=final)
    return pl.pallas_call(
        body,
        grid=(t // tm,),
        in_specs=[pl.BlockSpec((tm, d), lambda i: (i, 0)),
                  _resident((1, d)), _resident(wg3.shape), _resident(wu3.shape), _resident(wd3.shape),
                  _resident((1, d))],
        out_specs=pl.BlockSpec((tm, d), lambda i: (i, 0)),
        out_shape=jax.ShapeDtypeStruct((t, d), F32),
        scratch_shapes=[pltpu.VMEM((tm, d), BF16), pltpu.VMEM((tm, d), F32)],
        compiler_params=_cparams("parallel"),
        name="ffn",
    )(x, g, wg3, wu3, wd3, final_g)


def _proj_body(x_ref, g_ref, w_ref, *out_refs, n_main):
    xn = _rms(x_ref[...], g_ref[...]).astype(BF16)
    main_ref = out_refs[0]
    for j in range(0, n_main, PROJ_CHUNK):
        main_ref[:, j:j + PROJ_CHUNK] = jnp.dot(xn, w_ref[:, j:j + PROJ_CHUNK],
                                                preferred_element_type=F32).astype(BF16)
    if len(out_refs) > 1:
        out_refs[1][...] = jnp.dot(xn, w_ref[:, n_main:], preferred_element_type=F32)


def _proj(x, g, w, n_aux):
    t, d = x.shape
    n_main = w.shape[1] - n_aux
    tm = TOKEN_TILE
    out_specs = [pl.BlockSpec((tm, n_main), lambda i: (i, 0))]
    out_shape = [jax.ShapeDtypeStruct((t, n_main), BF16)]
    if n_aux:
        out_specs.append(pl.BlockSpec((tm, n_aux), lambda i: (i, 0)))
        out_shape.append(jax.ShapeDtypeStruct((t, n_aux), F32))
    return pl.pallas_call(
        functools.partial(_proj_body, n_main=n_main),
        grid=(t // tm,),
        in_specs=[pl.BlockSpec((tm, d), lambda i: (i, 0)), _resident((1, d)), _resident(w.shape)],
        out_specs=out_specs,
        out_shape=out_shape,
        compiler_params=_cparams("parallel"),
        name="norm_proj",
    )(x, g, w)


def _ssd_prep_body(p_ref, bias_ref, alog_ref, dt_ref, cs_ref, *, q):
    raw = p_ref[:, 0:LANES]
    dt = _softplus(raw + bias_ref[...])
    dta = dt * (-jnp.exp(alog_ref[...]))
    dt_ref[...] = dt
    incl_f, _ = _tri_masks(q, True)
    incl_b, _ = _tri_masks(q, False)
    tril = incl_f.astype(F32)
    triu = incl_b.astype(F32)
    lane = lax.broadcasted_iota(jnp.int32, (q, LANES), 1)
    n_heads = SSD_N_GROUPS * SSD_HEADS_PER_GROUP
    for c in range(raw.shape[0] // q):
        blk = dta[c * q:(c + 1) * q, :]
        cs_ref[c * q:(c + 1) * q, :] = jnp.where(lane < n_heads, _dot_f32(tril, blk), _dot_f32(triu, blk))


def _ssd_prep(p, bias, alog):
    t = p.shape[0]
    tm = TOKEN_TILE
    blk = pl.BlockSpec((tm, LANES), lambda i: (i, 0))
    return pl.pallas_call(
        functools.partial(_ssd_prep_body, q=SSD_CHUNK),
        grid=(t // tm,),
        in_specs=[pl.BlockSpec((tm, PROJ_AUX), lambda i: (i, 0)),
                  _resident((1, LANES)), _resident((1, LANES))],
        out_specs=[blk, blk],
        out_shape=[jax.ShapeDtypeStruct((t, LANES), F32)] * 2,
        compiler_params=_cparams("parallel"),
        name="ssd_prep",
    )(p, bias, alog)


def _ssd_core_body(xs_ref, bm_ref, cm_ref, wx_ref, wb_ref, wc_ref, bx_ref, bb_ref, bc_ref, dsk_ref,
                   dtr_ref, csr_ref, csc_ref, o_ref, xs_s, b_s, c_s, yf_s, yb_s, st_ref, *, q, n_chunks):
    hd, nh = SSD_HEAD_DIM, SSD_HEADS_PER_GROUP
    xs_s[...] = _dwconv_silu(xs_ref[...].astype(F32), wx_ref[...], bx_ref[...])
    b_s[...] = _dwconv_silu(bm_ref[...].astype(F32), wb_ref[...], bb_ref[...])
    c_s[...] = _dwconv_silu(cm_ref[...].astype(F32), wc_ref[...], bc_ref[...]).astype(BF16)
    st_ref[...] = jnp.zeros_like(st_ref)
    sh = int(math.log2(hd))
    lane_head = jnp.right_shift(lax.broadcasted_iota(jnp.int32, (q, nh * hd), 1), sh)
    lane_head_row = jnp.right_shift(lax.broadcasted_iota(jnp.int32, (1, nh * hd), 1), sh)

    def per_head(vals, lane_map):
        out = vals[nh - 1]
        for h in range(nh - 2, -1, -1):
            out = jnp.where(lane_map == h, vals[h], out)
        return out

    def step(i, carry):
        prep = []
        for d, forward in enumerate((True, False)):
            c = i if forward else n_chunks - 1 - i
            r0 = pl.multiple_of(c * q, q)
            incl, _ = _tri_masks(q, forward)
            cc = c_s[pl.ds(r0, q), :]
            bc = b_s[pl.ds(r0, q), :]
            cb = _dot_nt(cc, bc)
            bct = bc.T
            xc = xs_s[pl.ds(r0, q), :].astype(BF16)
            acols = csc_ref[pl.ds(r0, q), :]
            arows = csr_ref[c]
            dtrows = dtr_ref[c]
            lmats, wmats, eas, tots = [], [], [], []
            for h in range(nh):
                k = d * nh + h
                acol = jnp.broadcast_to(acols[:, k:k + 1], (q, q))
                arow = arows[k:k + 1, :]
                dtr = dtrows[k:k + 1, :]
                tot = arow[:, q - 1:q] if forward else arow[:, 0:1]
                lmats.append((jnp.where(incl, jnp.exp(acol - arow), 0.0) * cb * dtr).astype(BF16))
                wmats.append((bct * (jnp.exp(tot - arow) * dtr)).astype(BF16))
                eas.append(jnp.exp(jnp.concatenate([acol] * (nh * hd // q), axis=1)))
                tots.append(jnp.exp(tot))
            xms = [jnp.where(lane_head == h, xc, jnp.zeros_like(xc)) for h in range(nh)]
            prep.append((r0, cc, lmats, wmats, xms, per_head(eas, lane_head), per_head(tots, lane_head_row)))
        states = [st_ref[d] for d in range(2)]
        y_off = [jnp.dot(p[1], s.astype(BF16), preferred_element_type=F32) for p, s in zip(prep, states)]
        y_diag = [sum(jnp.dot(lm, xm, preferred_element_type=F32) for lm, xm in zip(p[2], p[4])) for p in prep]
        upd = [sum(jnp.dot(wm, xm, preferred_element_type=F32) for wm, xm in zip(p[3], p[4])) for p in prep]
        for d, y_s in enumerate((yf_s, yb_s)):
            y_s[pl.ds(prep[d][0], q), :] = y_diag[d] + y_off[d] * prep[d][5]
            st_ref[d] = states[d] * prep[d][6] + upd[d]
        return carry

    lax.fori_loop(0, n_chunks, step, 0)
    o_ref[...] = yf_s[...] + yb_s[...] + xs_s[...] * dsk_ref[...]


def _ssd_core(p3, conv_w, conv_b, dskip, dt_rows, cs_rows, cs_cols):
    b, l, _ = p3.shape
    q = SSD_CHUNK
    nc = l // q
    gw = SSD_HEADS_PER_GROUP * SSD_HEAD_DIM
    ns = SSD_D_STATE
    x0 = SSD_D_INNER // gw
    b0 = 2 * SSD_D_INNER // ns
    c0 = b0 + SSD_N_GROUPS
    wb0 = SSD_D_INNER // ns
    wc0 = wb0 + SSD_N_GROUPS
    rows_spec = pl.BlockSpec((None, None, nc, 8, q), lambda i, g: (i, g, 0, 0, 0))
    return pl.pallas_call(
        functools.partial(_ssd_core_body, q=q, n_chunks=nc),
        grid=(b, SSD_N_GROUPS),
        in_specs=[pl.BlockSpec((None, l, gw), lambda i, g: (i, 0, x0 + g)),
                  pl.BlockSpec((None, l, ns), lambda i, g: (i, 0, b0 + g)),
                  pl.BlockSpec((None, l, ns), lambda i, g: (i, 0, c0 + g)),
                  pl.BlockSpec((CONV_K, gw), lambda i, g: (0, g)),
                  pl.BlockSpec((CONV_K, ns), lambda i, g: (0, wb0 + g)),
                  pl.BlockSpec((CONV_K, ns), lambda i, g: (0, wc0 + g)),
                  pl.BlockSpec((1, gw), lambda i, g: (0, g)),
                  pl.BlockSpec((1, ns), lambda i, g: (0, wb0 + g)),
                  pl.BlockSpec((1, ns), lambda i, g: (0, wc0 + g)),
                  pl.BlockSpec((1, gw), lambda i, g: (0, g)),
                  rows_spec, rows_spec,
                  pl.BlockSpec((None, None, l, 8), lambda i, g: (i, g, 0, 0))],
        out_specs=pl.BlockSpec((None, l, gw), lambda i, g: (i, 0, g)),
        out_shape=jax.ShapeDtypeStruct((b, l, SSD_D_INNER), F32),
        scratch_shapes=[pltpu.VMEM((l, gw), F32), pltpu.VMEM((l, ns), F32), pltpu.VMEM((l, ns), BF16),
                        pltpu.VMEM((l, gw), F32), pltpu.VMEM((l, gw), F32), pltpu.VMEM((2, ns, gw), F32)],
        compiler_params=_cparams("parallel", "parallel"),
        name="ssd_core",
    )(p3, p3, p3, conv_w, conv_w, conv_w, conv_b, conv_b, conv_b, dskip, dt_rows, cs_rows, cs_cols)


def _ssd_out_body(y_ref, z_ref, nw_ref, w_ref, x_ref, o_ref):
    yz = y_ref[...] * _silu(z_ref[...].astype(F32))
    o_ref[...] = x_ref[...] + jnp.dot(_rms(yz, nw_ref[...]).astype(BF16), w_ref[...], preferred_element_type=F32)


def _ssd_out(y, p, norm_w, w_out, x):
    t, d = x.shape
    tm = TOKEN_TILE
    di = SSD_D_INNER
    return pl.pallas_call(
        _ssd_out_body,
        grid=(t // tm,),
        in_specs=[pl.BlockSpec((tm, di), lambda i: (i, 0)),
                  pl.BlockSpec((tm, di), lambda i: (i, 0)),
                  _resident((1, di)), _resident(w_out.shape),
                  pl.BlockSpec((tm, d), lambda i: (i, 0))],
        out_specs=pl.BlockSpec((tm, d), lambda i: (i, 0)),
        out_shape=jax.ShapeDtypeStruct((t, d), F32),
        compiler_params=_cparams("parallel"),
        name="ssd_out",
    )(y, p, norm_w, w_out, x)


def _group_rows_cols(a, b, l, nc, q):
    g = a.shape[2]
    cols = jnp.transpose(a, (0, 2, 1, 3))
    rows = jnp.transpose(a.reshape(b, nc, q, g, 8), (0, 3, 1, 4, 2))
    return rows, cols


def _ssd_mixer(x, b, l, g_norm, w_in, conv_w, conv_b, a_log, dt_bias, d_skip, norm_w, w_out):
    t = b * l
    q = SSD_CHUNK
    nc = l // q
    p, aux = _proj(x, g_norm, w_in, PROJ_AUX)
    dt, cs = _ssd_prep(aux, dt_bias, a_log)
    ng, nh = SSD_N_GROUPS, SSD_HEADS_PER_GROUP

    def regroup(a):
        a = a[:, :2 * ng * nh].reshape(b, l, 2, ng, nh)
        return jnp.transpose(a, (0, 1, 3, 2, 4)).reshape(b, l, ng, 2 * nh)

    dt_rows, _ = _group_rows_cols(regroup(dt), b, l, nc, q)
    cs_rows, cs_cols = _group_rows_cols(regroup(cs), b, l, nc, q)
    y = _ssd_core(p.reshape(b, l, -1), conv_w, conv_b, d_skip, dt_rows, cs_rows, cs_cols)
    return _ssd_out(y.reshape(t, SSD_D_INNER), p, norm_w, w_out, x)


def _gdn_prep_body(p_ref, bias_ref, alog_ref, o_ref, *, cs):
    raw = p_ref[:, 0:LANES]
    beta = _sigmoid(raw)
    g = -jnp.exp(alog_ref[...]) * _softplus(raw + bias_ref[...])
    n = LANES
    li = lax.broadcasted_iota(jnp.int32, (n, n), 0)
    si = lax.broadcasted_iota(jnp.int32, (n, n), 1)
    sh = int(math.log2(cs))
    same = jnp.right_shift(li, sh) == jnp.right_shift(si, sh)
    tril = (same & (li >= si)).astype(F32)
    triu = (same & (li <= si)).astype(F32)
    lane = lax.broadcasted_iota(jnp.int32, (n, LANES), 1)
    hv = GDN_N_V_HEADS
    for c in range(raw.shape[0] // n):
        blk = g[c * n:(c + 1) * n, :]
        o_ref[c * n:(c + 1) * n, :] = jnp.where(
            lane < hv, beta[c * n:(c + 1) * n, :],
            jnp.where(lane < 2 * hv, _dot_f32(tril, blk), _dot_f32(triu, blk)))


def _gdn_prep(p, bias, alog):
    t = p.shape[0]
    tm = TOKEN_TILE
    return pl.pallas_call(
        functools.partial(_gdn_prep_body, cs=GDN_CHUNK),
        grid=(t // tm,),
        in_specs=[pl.BlockSpec((tm, PROJ_AUX), lambda i: (i, 0)),
                  _resident((1, LANES)), _resident((1, LANES))],
        out_specs=pl.BlockSpec((tm, LANES), lambda i: (i, 0)),
        out_shape=jax.ShapeDtypeStruct((t, LANES), F32),
        compiler_params=_cparams("parallel"),
        name="gdn_prep",
    )(p, bias, alog)


def _gdn_core_body(q_ref, k_ref, v_ref, wq_ref, wk_ref, wv_ref, bq_ref, bk_ref, bv_ref, grow_ref, cols_ref,
                   o_ref, q_s, k_s, v_s, u_s, wq_s, qkk_s, yf_s, yb_s, st_ref, *, cs, n_chunks):
    hd = GDN_HEAD
    qn = _dwconv_silu(q_ref[...].astype(F32), wq_ref[...], bq_ref[...])
    qn = qn * lax.rsqrt(jnp.sum(qn * qn, axis=-1, keepdims=True) + 1e-6)
    q_s[...] = qn * (hd ** -0.5)
    kn = _dwconv_silu(k_ref[...].astype(F32), wk_ref[...], bk_ref[...])
    k_s[...] = kn * lax.rsqrt(jnp.sum(kn * kn, axis=-1, keepdims=True) + 1e-6)
    v_s[...] = _dwconv_silu(v_ref[...].astype(F32), wv_ref[...], bv_ref[...])
    st_ref[...] = jnp.zeros_like(st_ref)

    lane_blk = jnp.right_shift(lax.broadcasted_iota(jnp.int32, (cs, 4 * cs), 1), int(math.log2(cs)))
    t_row = lax.broadcasted_iota(jnp.int32, (cs, 4 * cs), 0)
    t_col = jnp.bitwise_and(lax.broadcasted_iota(jnp.int32, (cs, 4 * cs), 1), cs - 1)
    eye_packed = jnp.where(t_row == t_col, 1.0, 0.0)
    fwd_blk = lane_blk < 2
    incl_p = (fwd_blk & (t_row >= t_col)) | (jnp.logical_not(fwd_blk) & (t_row <= t_col))
    strict_p = (fwd_blk & (t_row > t_col)) | (jnp.logical_not(fwd_blk) & (t_row < t_col))
    head0 = lax.broadcasted_iota(jnp.int32, (cs, 2 * hd), 1) < hd
    head0_row = lax.broadcasted_iota(jnp.int32, (1, 2 * hd), 1) < hd
    half_row = lax.broadcasted_iota(jnp.int32, (1, 2 * cs), 1) < cs

    def block_diag(z):
        return jnp.concatenate([jnp.where(lane_blk == r, z, 0.0) for r in range(4)], axis=0).astype(BF16)

    def pick(cols, idx):
        out = jnp.where(lane_blk == 2, cols[:, idx[2]:idx[2] + 1], cols[:, idx[3]:idx[3] + 1])
        out = jnp.where(lane_blk == 1, cols[:, idx[1]:idx[1] + 1], out)
        return jnp.where(lane_blk == 0, cols[:, idx[0]:idx[0] + 1], out)

    def chunk_end(g, forward):
        return (g[:, cs - 1:cs], g[:, 2 * cs - 1:2 * cs]) if forward else (g[:, 0:1], g[:, cs:cs + 1])

    def factor(i, carry):
        packs = []
        for j in range(FACTOR_CHUNKS):
            c = i * FACTOR_CHUNKS + j
            r0 = pl.multiple_of(c * cs, cs)
            c3 = pl.multiple_of(c * 3 * cs, cs)
            c4 = pl.multiple_of(c * 4 * cs, 4 * cs)
            qc = q_s[pl.ds(r0, cs), :]
            kc = k_s[pl.ds(r0, cs), :]
            vc = v_s[pl.ds(r0, cs), :]
            k4 = jnp.concatenate([kc] * 4, axis=0)
            packs.append((r0, c3, c4, qc, kc, vc, cols_ref[pl.ds(r0, cs), :], grow_ref[c],
                          _dot_nt(kc, k4), _dot_nt(qc, k4)))
        n_p, rhs_all = [], []
        for r0, c3, c4, qc, kc, vc, cols, grow_p, kk4, qk4 in packs:
            dec_p = jnp.where(incl_p, jnp.exp(pick(cols, (2, 3, 4, 5)) - grow_p), 0.0)
            n_p.append(jnp.where(strict_p, -(pick(cols, (0, 1, 0, 1)) * kk4 * dec_p), 0.0))
            qkm = jnp.where(incl_p, qk4 * dec_p, 0.0)
            kt2 = jnp.concatenate([kc, kc], axis=0).T
            rhs = []
            for d, forward in enumerate((True, False)):
                g = grow_p[:, d * 2 * cs:(d + 1) * 2 * cs]
                gl0, gl1 = chunk_end(g, forward)
                qkk_s[d, pl.ds(c3, cs), :] = qkm[:, d * 2 * cs:(d + 1) * 2 * cs].astype(BF16)
                qkk_s[d, pl.ds(c3 + cs, 2 * cs), :] = (kt2 * jnp.exp(jnp.where(half_row, gl0, gl1) - g)).astype(BF16)
                for e in range(2):
                    bcol = cols[:, e:e + 1]
                    eg = jnp.exp(cols[:, 2 + 2 * d + e:3 + 2 * d + e])
                    rhs.append(jnp.concatenate([vc[:, e * hd:(e + 1) * hd] * bcol, kc * (bcol * eg)],
                                               axis=1).astype(BF16))
                    wq_s[d, pl.ds(c4 + (2 * e + 1) * cs, cs), :] = (qc * eg).astype(BF16)
            rhs_all.append(rhs)
        m_p = n_p
        inv_p = [eye_packed + m for m in n_p]
        steps = int(math.log2(cs))
        for s in range(1, steps):
            bds = [block_diag(m) for m in m_p]
            if s == 1:
                m_p = [jnp.dot(m.astype(BF16), bd, preferred_element_type=F32) for m, bd in zip(m_p, bds)]
                continue
            res = [jnp.dot(jnp.concatenate([p, m], axis=0).astype(BF16), bd, preferred_element_type=F32)
                   for p, m, bd in zip(inv_p, m_p, bds)]
            inv_p = [p + r[:cs, :] for p, r in zip(inv_p, res)]
            m_p = [r[cs:, :] for r in res]
        bds = [block_diag(m) for m in m_p]
        inv_p = [(p + jnp.dot(p.astype(BF16), bd, preferred_element_type=F32)).astype(BF16)
                 for p, bd in zip(inv_p, bds)]
        zero = jnp.zeros((cs, 2 * hd), BF16)
        sols = [[jnp.dot(p, jnp.concatenate([zero] * r + [rhs[r]] + [zero] * (3 - r), axis=0),
                         preferred_element_type=F32) for r in range(4)] for p, rhs in zip(inv_p, rhs_all)]
        for (r0, c3, c4, *_), sol in zip(packs, sols):
            for d in range(2):
                for e in range(2):
                    u_s[d, pl.ds(r0, cs), e * hd:(e + 1) * hd] = sol[2 * d + e][:, :hd]
                    wq_s[d, pl.ds(c4 + 2 * e * cs, cs), :] = sol[2 * d + e][:, hd:].astype(BF16)
        return carry

    def scan(i, carry):
        cidx = [i, n_chunks - 1 - i]
        r0s = [pl.multiple_of(c * cs, cs) for c in cidx]
        c3s = [pl.multiple_of(c * 3 * cs, cs) for c in cidx]
        c4s = [pl.multiple_of(c * 4 * cs, 4 * cs) for c in cidx]
        states = [st_ref[d] for d in range(2)]
        lv1 = [jnp.dot(wq_s[d, pl.ds(c4s[d], 4 * cs), :], states[d].astype(BF16), preferred_element_type=F32)
               for d in range(2)]
        vbd, qss = [], []
        for d in range(2):
            v_new = u_s[d, pl.ds(r0s[d], cs), :] - jnp.where(head0, lv1[d][:cs, :], lv1[d][2 * cs:3 * cs, :])
            vbd.append(jnp.concatenate([jnp.where(head0, v_new, 0.0), jnp.where(head0, 0.0, v_new)],
                                       axis=0).astype(BF16))
            qss.append(jnp.where(head0, lv1[d][cs:2 * cs, :], lv1[d][3 * cs:, :]))
        lv2 = [jnp.dot(qkk_s[d, pl.ds(c3s[d], 3 * cs), :], vbd[d], preferred_element_type=F32)
               for d in range(2)]
        for d, (forward, y_s) in enumerate(((True, yf_s), (False, yb_s))):
            gl0, gl1 = chunk_end(grow_ref[cidx[d]][:, d * 2 * cs:(d + 1) * 2 * cs], forward)
            y_s[pl.ds(r0s[d], cs), :] = qss[d] + lv2[d][:cs, :]
            st_ref[d] = states[d] * jnp.exp(jnp.where(head0_row, gl0, gl1)) + lv2[d][cs:, :]
        return carry

    lax.fori_loop(0, n_chunks // FACTOR_CHUNKS, factor, 0)
    lax.fori_loop(0, n_chunks, scan, 0)
    o_ref[...] = yf_s[...] + yb_s[...]


def _gdn_core(p3, conv_w, conv_b, grow4, cols):
    b, l, _ = p3.shape
    cs = GDN_CHUNK
    nc = l // cs
    hd = GDN_HEAD
    vw = 2 * hd
    k0 = GDN_QK_DIM // hd
    v0 = 2 * GDN_QK_DIM // vw
    assert l % (cs * FACTOR_CHUNKS) == 0, (l, cs, FACTOR_CHUNKS)
    return pl.pallas_call(
        functools.partial(_gdn_core_body, cs=cs, n_chunks=nc),
        grid=(b, GDN_N_K_HEADS),
        in_specs=[pl.BlockSpec((None, l, hd), lambda i, j: (i, 0, j)),
                  pl.BlockSpec((None, l, hd), lambda i, j: (i, 0, k0 + j)),
                  pl.BlockSpec((None, l, vw), lambda i, j: (i, 0, v0 + j)),
                  pl.BlockSpec((CONV_K, hd), lambda i, j: (0, j)),
                  pl.BlockSpec((CONV_K, hd), lambda i, j: (0, k0 + j)),
                  pl.BlockSpec((CONV_K, vw), lambda i, j: (0, v0 + j)),
                  pl.BlockSpec((1, hd), lambda i, j: (0, j)),
                  pl.BlockSpec((1, hd), lambda i, j: (0, k0 + j)),
                  pl.BlockSpec((1, vw), lambda i, j: (0, v0 + j)),
                  pl.BlockSpec((None, None, nc, 1, 4 * cs), lambda i, j: (i, j, 0, 0, 0)),
                  pl.BlockSpec((None, None, l, 8), lambda i, j: (i, j, 0, 0))],
        out_specs=pl.BlockSpec((None, l, vw), lambda i, j: (i, 0, j)),
        out_shape=jax.ShapeDtypeStruct((b, l, GDN_V_DIM), F32),
        scratch_shapes=[pltpu.VMEM((l, hd), F32), pltpu.VMEM((l, hd), F32), pltpu.VMEM((l, vw), F32),
                        pltpu.VMEM((2, l, vw), F32),
                        pltpu.VMEM((2, 4 * l, hd), BF16),
                        pltpu.VMEM((2, 3 * l, 2 * cs), BF16),
                        pltpu.VMEM((l, vw), F32), pltpu.VMEM((l, vw), F32), pltpu.VMEM((2, hd, vw), F32)],
        compiler_params=_cparams("parallel", "parallel"),
        name="gdn_core",
    )(p3, p3, p3, conv_w, conv_w, conv_w, conv_b, conv_b, conv_b, grow4, cols)


def _gdn_out_body(o_ref, z_ref, nw_ref, w_ref, x_ref, out_ref, h_s):
    hd = GDN_HEAD
    nw = nw_ref[...]
    for e in range(GDN_N_V_HEADS):
        oe = o_ref[:, e * hd:(e + 1) * hd]
        ze = z_ref[:, e * hd:(e + 1) * hd].astype(F32)
        h_s[:, e * hd:(e + 1) * hd] = (_rms(oe, nw) * _silu(ze)).astype(BF16)
    out_ref[...] = x_ref[...] + jnp.dot(h_s[...], w_ref[...], preferred_element_type=F32)


def _gdn_out(o, p, norm_w, w_out, x):
    t, d = x.shape
    tm = TOKEN_TILE
    dv = GDN_V_DIM
    return pl.pallas_call(
        _gdn_out_body,
        grid=(t // tm,),
        in_specs=[pl.BlockSpec((tm, dv), lambda i: (i, 0)),
                  pl.BlockSpec((tm, dv), lambda i: (i, 2 * GDN_QK_DIM // dv + 1)),
                  _resident((1, GDN_HEAD)), _resident(w_out.shape),
                  pl.BlockSpec((tm, d), lambda i: (i, 0))],
        out_specs=pl.BlockSpec((tm, d), lambda i: (i, 0)),
        out_shape=jax.ShapeDtypeStruct((t, d), F32),
        scratch_shapes=[pltpu.VMEM((tm, dv), BF16)],
        compiler_params=_cparams("parallel"),
        name="gdn_out",
    )(o, p, norm_w, w_out, x)


def _gdn_mixer(x, b, l, g_norm, w_in, conv_w, conv_b, a_log, dt_bias, norm_w, w_out):
    t = b * l
    cs = GDN_CHUNK
    nc = l // cs
    p, aux = _proj(x, g_norm, w_in, PROJ_AUX)
    gt = _gdn_prep(aux, dt_bias, a_log).reshape(b, l, LANES)
    hk, hv = GDN_N_K_HEADS, GDN_N_V_HEADS
    parts = [gt[..., i * hv:(i + 1) * hv].reshape(b, l, hk, 2) for i in range(3)]
    a = jnp.concatenate(parts + [jnp.zeros((b, l, hk, 2), F32)], axis=-1)
    cols = jnp.transpose(a, (0, 2, 1, 3))
    grow4 = jnp.transpose(a[..., 2:6].reshape(b, nc, cs, hk, 4), (0, 3, 1, 4, 2)).reshape(b, hk, nc, 1, 4 * cs)
    o = _gdn_core(p.reshape(b, l, -1), conv_w, conv_b, grow4, cols)
    return _gdn_out(o.reshape(t, GDN_V_DIM), p, norm_w, w_out, x)


def _att_core_body(q_ref, k_ref, v_ref, cos_ref, sin_ref, sink_ref, o_ref, q_s, k_s, v_s, band_s,
                   *, win, n_blocks, seq):
    dh = ATT_HEAD_DIM
    cos = cos_ref[...]
    sin = sin_ref[...]

    def rope(x):
        w = x.shape[1]
        lane = lax.broadcasted_iota(jnp.int32, x.shape, 1)
        first = jnp.bitwise_and(lane, dh - 1) < dh // 2
        partner = jnp.where(first, pltpu.roll(x, w - dh // 2, axis=1), pltpu.roll(x, dh // 2, axis=1))
        reps = w // LANES
        c = jnp.concatenate([cos] * reps, axis=1) if reps > 1 else cos
        s = jnp.concatenate([sin] * reps, axis=1) if reps > 1 else sin
        return x * c + partner * s

    n_heads = q_ref.shape[1] // dh
    rep = ATT_N_HEADS // ATT_N_KV_HEADS
    groups = range(n_heads // rep)
    qr = rope(q_ref[...].astype(F32)) * (dh ** -0.5)
    for h in range(n_heads):
        q_s[h] = qr[:, h * dh:(h + 1) * dh].astype(BF16)
    kr = rope(k_ref[...].astype(F32))
    vv = v_ref[...]
    ones = jnp.ones((seq, dh), BF16)
    for g in groups:
        for ref in (k_s, v_s):
            ref[g, pl.ds(0, win), :] = jnp.zeros((win, ref.shape[2]), BF16)
            ref[g, pl.ds(win + seq, win), :] = jnp.zeros((win, ref.shape[2]), BF16)
        k_s[g, pl.ds(win, seq), :] = kr[:, g * dh:(g + 1) * dh].astype(BF16)
        v_s[g, pl.ds(win, seq), :] = jnp.concatenate([vv[:, g * dh:(g + 1) * dh], ones], axis=1)
    qi = lax.broadcasted_iota(jnp.int32, (win, 3 * win), 0)
    ki = lax.broadcasted_iota(jnp.int32, (win, 3 * win), 1)
    band_s[...] = jnp.where(jnp.abs(ki - win - qi) <= win, 0.0, -jnp.inf)
    sinks = sink_ref[...]

    def block(n, carry):
        r0 = pl.multiple_of(n * win, win)
        kpos = r0 - win + lax.broadcasted_iota(jnp.int32, (1, 3 * win), 1)
        bias = band_s[...] + jnp.where((kpos >= 0) & (kpos < seq), 0.0, -jnp.inf)
        scores = []
        for g in groups:
            qg = jnp.concatenate([q_s[g * rep + r, pl.ds(r0, win), :] for r in range(rep)], axis=0)
            scores.append(_dot_nt(qg, k_s[g, pl.ds(r0, 3 * win), :]))
        exps, maxes = [], []
        for g in groups:
            es = []
            for r in range(rep):
                s = scores[g][r * win:(r + 1) * win, :] + bias
                sink = sinks[:, g * rep + r:g * rep + r + 1]
                m = jnp.maximum(jnp.max(s, axis=-1, keepdims=True), sink)
                es.append(jnp.exp(s - m).astype(BF16))
                maxes.append(m)
            exps.append(jnp.concatenate(es, axis=0))
        nds = [jnp.dot(exps[g], v_s[g, pl.ds(r0, 3 * win), :], preferred_element_type=F32) for g in groups]
        outs = []
        for g in groups:
            for r in range(rep):
                h = g * rep + r
                nd = nds[g][r * win:(r + 1) * win, :]
                outs.append(nd[:, :dh] / (nd[:, dh:] + jnp.exp(sinks[:, h:h + 1] - maxes[h])))
        o_ref[pl.ds(r0, win), :] = jnp.concatenate(outs, axis=1).astype(o_ref.dtype)
        return carry

    lax.fori_loop(0, n_blocks, block, 0)


def _att_core(p3, cos2, sin2, sinks3):
    b, l, _ = p3.shape
    win = ATT_WINDOW
    dh = ATT_HEAD_DIM
    steps = ATT_N_KV_HEADS * dh // LANES
    qw = ATT_N_HEADS * dh // steps
    k0 = ATT_N_HEADS * dh // LANES
    v0 = k0 + steps
    return pl.pallas_call(
        functools.partial(_att_core_body, win=win, n_blocks=l // win, seq=l),
        grid=(b, steps),
        in_specs=[pl.BlockSpec((None, l, qw), lambda i, j: (i, 0, j)),
                  pl.BlockSpec((None, l, LANES), lambda i, j: (i, 0, k0 + j)),
                  pl.BlockSpec((None, l, LANES), lambda i, j: (i, 0, v0 + j)),
                  _resident((l, LANES)), _resident((l, LANES)),
                  pl.BlockSpec((None, 1, qw // dh), lambda i, j: (j, 0, 0))],
        out_specs=pl.BlockSpec((None, l, qw), lambda i, j: (i, 0, j)),
        out_shape=jax.ShapeDtypeStruct((b, l, ATT_N_HEADS * dh), BF16),
        scratch_shapes=[pltpu.VMEM((qw // dh, l, dh), BF16),
                        pltpu.VMEM((LANES // dh, l + 2 * win, dh), BF16),
                        pltpu.VMEM((LANES // dh, l + 2 * win, 2 * dh), BF16),
                        pltpu.VMEM((win, 3 * win), F32)],
        compiler_params=_cparams("parallel", "parallel"),
        name="att_core",
    )(p3, p3, p3, cos2, sin2, sinks3)


def _res_proj_body(a_ref, w_ref, x_ref, o_ref):
    o_ref[...] = x_ref[...] + jnp.dot(a_ref[...], w_ref[...], preferred_element_type=F32)


def _res_proj(a, w, x):
    t, d = x.shape
    k = a.shape[1]
    tm = TOKEN_TILE
    return pl.pallas_call(
        _res_proj_body,
        grid=(t // tm,),
        in_specs=[pl.BlockSpec((tm, k), lambda i: (i, 0)), _resident(w.shape),
                  pl.BlockSpec((tm, d), lambda i: (i, 0))],
        out_specs=pl.BlockSpec((tm, d), lambda i: (i, 0)),
        out_shape=jax.ShapeDtypeStruct((t, d), F32),
        compiler_params=_cparams("parallel"),
        name="res_proj",
    )(a, w, x)


def _att_mixer(x, b, l, g_norm, w_qkv, cos2, sin2, sinks3, w_out):
    t = b * l
    (p,) = _proj(x, g_norm, w_qkv, 0)
    o = _att_core(p.reshape(b, l, -1), cos2, sin2, sinks3)
    return _res_proj(o.reshape(t, -1), w_out, x)


def _rwkv_prep_body(x_ref, xp_ref, xn_ref, g_ref, mu_ref, wr_ref, wk_ref, wv_ref, w1_ref, w2_ref, w0_ref,
                    a0_ref, a1_ref, a2_ref, g1_ref, g2_ref, kk_ref, ka_ref,
                    r_o, k_o, v_o, kk_o, a_o, lw0_o, lw1_o, gate_o, *, n_tiles):
    i = pl.program_id(1)
    g = g_ref[...]
    u = _rms(x_ref[...], g)
    tl = u.shape[0]
    prev_row = jnp.where(i > 0, _rms(xp_ref[...], g)[7:8, :], 0.0)
    next_row = jnp.where(i < n_tiles - 1, _rms(xn_ref[...], g)[0:1, :], 0.0)
    row = lax.broadcasted_iota(jnp.int32, u.shape, 0)
    u_prev = jnp.where(row == 0, prev_row, pltpu.roll(u, 1, axis=0))
    u_next = jnp.where(row == tl - 1, next_row, pltpu.roll(u, tl - 1, axis=0))
    xx = 0.5 * (u_prev + u_next) - u
    mu = mu_ref[...]

    def mix(s):
        return (u + xx * mu[s:s + 1, :]).astype(BF16)

    r_o[...] = jnp.dot(mix(0), wr_ref[...], preferred_element_type=F32)
    k = jnp.dot(mix(1), wk_ref[...], preferred_element_type=F32)
    v_o[...] = jnp.dot(mix(2), wv_ref[...], preferred_element_type=F32)
    wl = jnp.tanh(jnp.dot(mix(3), w1_ref[...], preferred_element_type=F32))
    lora = w2_ref.shape[1]
    w0 = w0_ref[...]
    for d, out in enumerate((lw0_o, lw1_o)):
        z = w0[d:d + 1, :] + _dot(wl[:, d * lora:(d + 1) * lora], w2_ref[d])
        out[...] = -jnp.exp(-_softplus(-z) - 0.5)
    a = _sigmoid(a0_ref[...] + _dot(jnp.dot(mix(4), a1_ref[...], preferred_element_type=F32), a2_ref[...]))
    gate_o[...] = _dot(_sigmoid(jnp.dot(mix(5), g1_ref[...], preferred_element_type=F32)), g2_ref[...])
    a_o[...] = a
    kk_o[...] = k * kk_ref[...]
    k_o[...] = k * (1.0 + (a - 1.0) * ka_ref[...])


def _rwkv_prep(x3, g, mu, wr, wk, wv, w1, w2, w0, a0, a1, a2, g1, g2, k_k, k_a):
    b, l, d = x3.shape
    tl = 256
    nt = l // tl
    x4 = x3.reshape(b, l // 8, 8, d)
    tile = pl.BlockSpec((None, tl, d), lambda i, j: (i, j, 0))
    consts = [g, mu, wr, wk, wv, w1, w2, w0, a0, a1, a2, g1, g2, k_k, k_a]
    return pl.pallas_call(
        functools.partial(_rwkv_prep_body, n_tiles=nt),
        grid=(b, nt),
        in_specs=[tile,
                  pl.BlockSpec((None, None, 8, d), lambda i, j: (i, jnp.maximum(j * (tl // 8) - 1, 0), 0, 0)),
                  pl.BlockSpec((None, None, 8, d),
                               lambda i, j: (i, jnp.minimum((j + 1) * (tl // 8), l // 8 - 1), 0, 0))]
                 + [_resident(c.shape) for c in consts],
        out_specs=[tile] * 8,
        out_shape=[jax.ShapeDtypeStruct((b, l, d), F32)] * 8,
        compiler_params=_cparams("parallel", "parallel"),
        name="rwkv_prep",
    )(x3, x4, x4, *consts)


def _rwkv_core_body(r_ref, k_ref, v_ref, kk_ref, a_ref, lw0_ref, lw1_ref, rk_ref, lnw_ref, lnb_ref, o_ref,
                    kk_s, kb_s, wr_s, u0_s, lv2_s, pe_s, yf_s, yb_s, st_ref, *, cs, n_chunks):
    n = RWKV_HEAD
    kk = kk_ref[...]
    kk = kk * lax.rsqrt(_seg_sum2(kk * kk) + 1e-6)
    kk_s[...] = kk
    kb_s[...] = kk * a_ref[...]
    st_ref[...] = jnp.zeros_like(st_ref)

    lane_blk = jnp.right_shift(lax.broadcasted_iota(jnp.int32, (cs, 4 * cs), 1), int(math.log2(cs)))
    t_row = lax.broadcasted_iota(jnp.int32, (cs, 4 * cs), 0)
    t_col = jnp.bitwise_and(lax.broadcasted_iota(jnp.int32, (cs, 4 * cs), 1), cs - 1)
    eye_packed = jnp.where(t_row == t_col, 1.0, 0.0)
    head0 = lax.broadcasted_iota(jnp.int32, (cs, LANES), 1) < n
    head0_rows = lax.broadcasted_iota(jnp.int32, (LANES, cs), 0) < n

    def block_diag(z):
        return jnp.concatenate([jnp.where(lane_blk == r, z, 0.0) for r in range(4)], axis=0).astype(BF16)

    def split_heads_rows(z):
        return jnp.concatenate([jnp.where(head0, z, 0.0), jnp.where(head0, 0.0, z)], axis=0)

    def factor(i, carry):
        span = FACTOR_CHUNKS * cs
        base = pl.multiple_of(i * span, span)
        rows = pl.ds(base, span)
        rc2, kc2, vc2, kkc2, kbc2 = r_ref[rows, :], k_ref[rows, :], v_ref[rows, :], kk_s[rows, :], kb_s[rows, :]
        li = lax.broadcasted_iota(jnp.int32, (2 * cs, 2 * cs), 0)
        si = lax.broadcasted_iota(jnp.int32, (2 * cs, 2 * cs), 1)
        sh = int(math.log2(cs))
        same = jnp.right_shift(li, sh) == jnp.right_shift(si, sh)
        lws, cums = [], []
        for forward, lw_ref in ((True, lw0_ref), (False, lw1_ref)):
            lws.append(lw_ref[rows, :])
            tri = (same & ((li >= si) if forward else (li <= si))).astype(F32)
            cums.append(jnp.concatenate([_dot_f32(tri, lws[-1][p * 2 * cs:(p + 1) * 2 * cs, :])
                                         for p in range(FACTOR_CHUNKS // 2)], axis=0))
        units, lhss, rhss = [], [], []
        for j in range(FACTOR_CHUNKS):
            rs = slice(j * cs, (j + 1) * cs)
            c = i * FACTOR_CHUNKS + j
            r0 = pl.multiple_of(c * cs, cs)
            c2 = pl.multiple_of(c * 2 * cs, 2 * cs)
            c3 = pl.multiple_of(c * 3 * cs, cs)
            for d, forward in enumerate((True, False)):
                lw, cum = lws[d][rs, :], cums[d][rs, :]
                tot = cum[cs - 1:cs, :] if forward else cum[0:1, :]
                e_neg = jnp.exp(-cum)
                e_end = jnp.exp(tot - cum)
                a_hat = -kkc2[rs, :] * jnp.exp(cum - lw)
                r_hat = rc2[rs, :] * jnp.exp(cum)
                wr_s[d, pl.ds(c2 + cs, cs), :] = r_hat.astype(BF16)
                b_end_t = (kbc2[rs, :] * e_end).T
                k_end_t = (kc2[rs, :] * e_end).T
                lv2_s[d, pl.ds(c3 + cs, 2 * cs), :] = jnp.concatenate(
                    [jnp.where(head0_rows, b_end_t, 0.0), jnp.where(head0_rows, 0.0, b_end_t),
                     jnp.where(head0_rows, k_end_t, 0.0), jnp.where(head0_rows, 0.0, k_end_t)], axis=1).astype(BF16)
                p_col = jnp.broadcast_to(jnp.exp(tot), (8, LANES)).T[:, 0:1]
                pe_s[d, pl.ds(c2, 2 * cs), :] = jnp.broadcast_to(p_col, (LANES, LANES))
                lhss.append(jnp.concatenate([a_hat, r_hat], axis=0))
                rhss.append(jnp.concatenate([split_heads_rows(kbc2[rs, :] * e_neg),
                                             split_heads_rows(kc2[rs, :] * e_neg)], axis=0))
                units.append((j, d, forward, r0, c2, c3, a_hat))
        grams = [_dot_nt(lh, rh) for lh, rh in zip(lhss, rhss)]
        tops = []
        for (j, d, forward, r0, c2, c3, _), gram in zip(units, grams):
            strict = (t_row > t_col) if forward else (t_row < t_col)
            incl = (t_row >= t_col) if forward else (t_row <= t_col)
            tops.append(jnp.where(strict, gram[:cs, :], 0.0))
            lv2_s[d, pl.ds(c3, cs), :] = jnp.where(incl, gram[cs:, :], 0.0).astype(BF16)
        n_p = [jnp.concatenate([tops[2 * j][:, :LANES], tops[2 * j + 1][:, :LANES]], axis=1)
               for j in range(FACTOR_CHUNKS)]
        ak_p = [jnp.concatenate([tops[2 * j][:, LANES:], tops[2 * j + 1][:, LANES:]], axis=1)
                for j in range(FACTOR_CHUNKS)]
        a_p = [jnp.concatenate([units[2 * j][6], units[2 * j + 1][6]], axis=1) for j in range(FACTOR_CHUNKS)]
        v_bd = [block_diag(jnp.concatenate([vc2[j * cs:(j + 1) * cs, :]] * 2, axis=1))
                for j in range(FACTOR_CHUNKS)]
        av_p = [jnp.dot(ak.astype(BF16), vb, preferred_element_type=F32) for ak, vb in zip(ak_p, v_bd)]
        m_p = n_p
        inv_p = [eye_packed + m for m in n_p]
        steps = int(math.log2(cs))
        for s in range(1, steps):
            bds = [block_diag(m) for m in m_p]
            if s == 1:
                m_p = [jnp.dot(m.astype(BF16), bd, preferred_element_type=F32) for m, bd in zip(m_p, bds)]
                continue
            res = [jnp.dot(jnp.concatenate([p, m], axis=0).astype(BF16), bd, preferred_element_type=F32)
                   for p, m, bd in zip(inv_p, m_p, bds)]
            inv_p = [p + r[:cs, :] for p, r in zip(inv_p, res)]
            m_p = [r[cs:, :] for r in res]
        bds = [block_diag(m) for m in m_p]
        inv_p = [p + jnp.dot(p.astype(BF16), bd, preferred_element_type=F32) for p, bd in zip(inv_p, bds)]
        sols = [jnp.dot(p.astype(BF16), jnp.concatenate([block_diag(a), block_diag(av)], axis=1),
                        preferred_element_type=F32) for p, a, av in zip(inv_p, a_p, av_p)]
        for j in range(FACTOR_CHUNKS):
            for d in range(2):
                _, _, _, r0, c2, _, _ = units[2 * j + d]
                wr_s[d, pl.ds(c2, cs), :] = sols[j][:, d * LANES:(d + 1) * LANES].astype(BF16)
                u0_s[d, pl.ds(r0, cs), :] = sols[j][:, (2 + d) * LANES:(3 + d) * LANES]
        return carry

    def scan(i, carry):
        cidx = [i, n_chunks - 1 - i]
        r0s = [pl.multiple_of(c * cs, cs) for c in cidx]
        c2s = [pl.multiple_of(c * 2 * cs, 2 * cs) for c in cidx]
        c3s = [pl.multiple_of(c * 3 * cs, cs) for c in cidx]
        states = [st_ref[d] for d in range(2)]
        lv1 = [jnp.dot(wr_s[d, pl.ds(c2s[d], 2 * cs), :], states[d].astype(BF16), preferred_element_type=F32)
               for d in range(2)]
        uvs = [jnp.concatenate([split_heads_rows(u0_s[d, pl.ds(r0s[d], cs), :] + lv1[d][:cs, :]),
                                split_heads_rows(v_ref[pl.ds(r0s[d], cs), :])], axis=0).astype(BF16)
               for d in range(2)]
        lv2 = [jnp.dot(lv2_s[d, pl.ds(c3s[d], 3 * cs), :], uvs[d], preferred_element_type=F32)
               for d in range(2)]
        for d, y_s in enumerate((yf_s, yb_s)):
            y_s[pl.ds(r0s[d], cs), :] = lv1[d][cs:, :] + lv2[d][:cs, :]
            st_ref[d] = states[d] * pe_s[d, pl.ds(c2s[d], 2 * cs), :] + lv2[d][cs:, :]
        return carry

    lax.fori_loop(0, n_chunks // FACTOR_CHUNKS, factor, 0)
    lax.fori_loop(0, n_chunks, scan, 0)
    o = yf_s[...] + yb_s[...]
    mean = _seg_sum2(o) * (1.0 / n)
    cen = o - mean
    var = _seg_sum2(cen * cen) * (1.0 / n)
    y = cen * lax.rsqrt(var + RWKV_LN_EPS) * lnw_ref[...] + lnb_ref[...]
    bonus = _seg_sum2(r_ref[...] * k_ref[...] * rk_ref[...]) * v_ref[...]
    o_ref[...] = y + bonus


def _rwkv_core(r, k, v, kk, a, lw0, lw1, r_k, lnw, lnb):
    b, l, d = r.shape
    cs = RWKV_CHUNK
    pair = pl.BlockSpec((None, l, LANES), lambda i, j: (i, 0, j))
    vec = pl.BlockSpec((1, LANES), lambda i, j: (0, j))
    assert l % (cs * FACTOR_CHUNKS) == 0, (l, cs, FACTOR_CHUNKS)
    return pl.pallas_call(
        functools.partial(_rwkv_core_body, cs=cs, n_chunks=l // cs),
        grid=(b, d // LANES),
        in_specs=[pair] * 7 + [vec] * 3,
        out_specs=pair,
        out_shape=jax.ShapeDtypeStruct((b, l, d), F32),
        scratch_shapes=[pltpu.VMEM((l, LANES), F32), pltpu.VMEM((l, LANES), F32),
                        pltpu.VMEM((2, 2 * l, LANES), BF16),
                        pltpu.VMEM((2, l, LANES), F32),
                        pltpu.VMEM((2, 3 * l, 2 * LANES), BF16),
                        pltpu.VMEM((2, 2 * l, LANES), F32),
                        pltpu.VMEM((l, LANES), F32), pltpu.VMEM((l, LANES), F32),
                        pltpu.VMEM((2, LANES, LANES), F32)],
        compiler_params=_cparams("parallel", "parallel"),
        name="rwkv_core",
    )(r, k, v, kk, a, lw0, lw1, r_k, lnw, lnb)


def _gate_proj_body(y_ref, gate_ref, w_ref, x_ref, o_ref):
    o_ref[...] = x_ref[...] + jnp.dot((y_ref[...] * gate_ref[...]).astype(BF16), w_ref[...],
                                      preferred_element_type=F32)


def _gate_proj(y, gate, w, x):
    t, d = x.shape
    tm = TOKEN_TILE
    tile = pl.BlockSpec((tm, d), lambda i: (i, 0))
    return pl.pallas_call(
        _gate_proj_body,
        grid=(t // tm,),
        in_specs=[tile, tile, _resident(w.shape), tile],
        out_specs=tile,
        out_shape=jax.ShapeDtypeStruct((t, d), F32),
        compiler_params=_cparams("parallel"),
        name="rwkv_out",
    )(y, gate, w, x)


def _rwkv_mixer(x, b, l, g_norm, mu, wr, wk, wv, w1, w2, w0, a0, a1, a2, g1, g2, k_k, k_a, r_k, lnw, lnb, w_out):
    t, d = x.shape
    r, k, v, kk, a, lw0, lw1, gate = _rwkv_prep(x.reshape(b, l, d), g_norm, mu, wr, wk, wv, w1, w2, w0,
                                                a0, a1, a2, g1, g2, k_k, k_a)
    y = _rwkv_core(r, k, v, kk, a, lw0, lw1, r_k, lnw, lnb)
    return _gate_proj(y.reshape(t, d), gate.reshape(t, d), w_out, x)


def _row(v):
    return v.reshape(1, -1).astype(F32)


def _pad_cols(w, n):
    return jnp.pad(w, ((0, 0), (0, n - w.shape[1])))


def _pad_row(v, n):
    v = v.reshape(1, -1).astype(F32)
    return jnp.pad(v, ((0, 0), (0, n - v.shape[1])))


def _ffn_weights(w_gu, w_down):
    d, f2 = w_gu.shape
    f = f2 // 2
    nck = f // FFN_CHUNK
    wg3 = jnp.transpose(w_gu[:, :f].reshape(d, nck, FFN_CHUNK), (1, 0, 2)).astype(BF16)
    wu3 = jnp.transpose(w_gu[:, f:].reshape(d, nck, FFN_CHUNK), (1, 0, 2)).astype(BF16)
    wd3 = w_down.reshape(nck, FFN_CHUNK, d).astype(BF16)
    return wg3, wu3, wd3


def _rope_tables(l):
    half = ATT_HEAD_DIM // 2
    inv_freq = ROPE_THETA ** (-jnp.arange(half, dtype=F32) / half)
    ang = jnp.arange(l).astype(F32)[:, None] * inv_freq[None, :]
    cos, sin = jnp.cos(ang), jnp.sin(ang)
    reps = LANES // ATT_HEAD_DIM
    cos2 = jnp.tile(jnp.concatenate([cos, cos], axis=1), (1, reps))
    sin2 = jnp.tile(jnp.concatenate([-sin, sin], axis=1), (1, reps))
    return cos2, sin2


def kernel(x_prompt, x_sample, ffn1_norm, ffn1_w_gu, ffn1_w_down, mix_norm, ffn2_norm, ffn2_w_gu, ffn2_w_down, ssd_w_in, ssd_conv_w, ssd_conv_b, ssd_a_log, ssd_dt_bias, ssd_d, ssd_norm, ssd_w_out, gdn_w_in, gdn_conv_w, gdn_conv_b, gdn_a_log, gdn_dt_bias, gdn_norm, gdn_w_out, att_w_qkv, att_sinks, att_w_out, rwkv_x_mu, rwkv_w_rkv, rwkv_w0, rwkv_w1, rwkv_w2, rwkv_a0, rwkv_a1, rwkv_a2, rwkv_g1, rwkv_g2, rwkv_k_k, rwkv_k_a, rwkv_r_k, rwkv_lnx_w, rwkv_lnx_b, rwkv_w_out, final_norm):
    depth = ffn1_norm.shape[0]
    ffn1 = [_ffn_weights(ffn1_w_gu[i], ffn1_w_down[i]) for i in range(depth)]
    ffn2 = [_ffn_weights(ffn2_w_gu[i], ffn2_w_down[i]) for i in range(depth)]
    fin_g = _row(final_norm)
    hv = GDN_N_V_HEADS
    att_steps = ATT_N_KV_HEADS * ATT_HEAD_DIM // LANES

    def mixer(h, b, l, i):
        m, j = i % N_MIXERS, i // N_MIXERS
        g = _row(mix_norm[i])
        if m == 0:
            return _ssd_mixer(h, b, l, g, _pad_cols(ssd_w_in[j], PROJ_PAD).astype(BF16), ssd_conv_w[j],
                              _row(ssd_conv_b[j]), _pad_row(ssd_a_log[j], LANES), _pad_row(ssd_dt_bias[j], LANES),
                              _row(jnp.repeat(ssd_d[j], SSD_HEAD_DIM)), _row(ssd_norm[j]), ssd_w_out[j].astype(BF16))
        if m == 1:
            lead = jnp.zeros((1, hv), F32)
            alog = jnp.pad(jnp.concatenate([lead, _row(gdn_a_log[j])], axis=1), ((0, 0), (0, LANES - 3 * hv)))
            bias = jnp.pad(jnp.concatenate([lead, _row(gdn_dt_bias[j])], axis=1), ((0, 0), (0, LANES - 3 * hv)))
            return _gdn_mixer(h, b, l, g, _pad_cols(gdn_w_in[j], PROJ_PAD).astype(BF16), gdn_conv_w[j],
                              _row(gdn_conv_b[j]), alog, bias, _row(gdn_norm[j]), gdn_w_out[j].astype(BF16))
        if m == 2:
            cos2, sin2 = _rope_tables(l)
            return _att_mixer(h, b, l, g, att_w_qkv[j].astype(BF16), cos2, sin2,
                              att_sinks[j].astype(F32).reshape(att_steps, 1, -1), att_w_out[j].astype(BF16))
        w1 = jnp.concatenate([rwkv_w1[j, 0], rwkv_w1[j, 1]], axis=1).astype(BF16)
        return _rwkv_mixer(h, b, l, g, rwkv_x_mu[j], rwkv_w_rkv[j, 0].astype(BF16), rwkv_w_rkv[j, 1].astype(BF16),
                           rwkv_w_rkv[j, 2].astype(BF16), w1, rwkv_w2[j].astype(BF16), rwkv_w0[j],
                           _row(rwkv_a0[j]), rwkv_a1[j].astype(BF16), rwkv_a2[j].astype(BF16),
                           rwkv_g1[j].astype(BF16), rwkv_g2[j].astype(BF16), _row(rwkv_k_k[j]), _row(rwkv_k_a[j]),
                           _row(rwkv_r_k[j]), _row(rwkv_lnx_w[j]), _row(rwkv_lnx_b[j]), rwkv_w_out[j].astype(BF16))

    def trunk(x3):
        b, l, d = x3.shape
        x = x3.reshape(b * l, d)
        for i in range(depth):
            x = _ffn(x, _row(ffn1_norm[i]), *ffn1[i], fin_g, False)
            x = mixer(x, b, l, i)
            x = _ffn(x, _row(ffn2_norm[i]), *ffn2[i], fin_g, i == depth - 1)
        return x.reshape(b, l, d)

    return (trunk(x_prompt), trunk(x_sample))
```

```python
import functools
import math

import jax
import jax.numpy as jnp
from jax import lax
from jax.experimental import pallas as pl
from jax.experimental.pallas import tpu as pltpu

F32 = jnp.float32
BF16 = jnp.bfloat16

V7X_VMEM_LIMIT_BYTES = 56 * 1024 * 1024
LANES = 128

RMS_EPS = 1e-6
CONV_K = 5
N_MIXERS = 4

SSD_HEAD_DIM = 64
SSD_N_GROUPS = 8
SSD_HEADS_PER_GROUP = 4
SSD_D_STATE = 128
SSD_CHUNK = 128
SSD_D_INNER = 2048

GDN_N_K_HEADS = 8
GDN_N_V_HEADS = 16
GDN_HEAD = 128
GDN_CHUNK = 64
GDN_QK_DIM = 1024
GDN_V_DIM = 2048

ATT_N_HEADS = 16
ATT_N_KV_HEADS = 4
ATT_HEAD_DIM = 64
ATT_WINDOW = 128
ROPE_THETA = 10000.0

RWKV_HEAD = 64
RWKV_CHUNK = 64
RWKV_LN_EPS = 64e-5

PROJ_AUX = 256
PROJ_PAD = 6144 + PROJ_AUX
PROJ_CHUNK = 512

TOKEN_TILE = 512
FFN_TOKEN_TILE = 1024
FFN_CHUNK = 256
FACTOR_CHUNKS = 8


def _cparams(*sem):
    return pltpu.CompilerParams(dimension_semantics=sem, vmem_limit_bytes=V7X_VMEM_LIMIT_BYTES)


def _resident(shape):
    nd = len(shape)
    return pl.BlockSpec(shape, lambda *_: (0,) * nd, pipeline_mode=pl.Buffered(1))


def _sigmoid(x):
    return 1.0 / (1.0 + jnp.exp(-x))


def _silu(x):
    return x * _sigmoid(x)


def _softplus(x):
    return jnp.maximum(x, 0.0) + jnp.log1p(jnp.exp(-jnp.abs(x)))


def _rms(x, g):
    return x * lax.rsqrt(jnp.mean(x * x, axis=-1, keepdims=True) + RMS_EPS) * g


def _dot(a, b):
    return jnp.dot(a.astype(BF16), b.astype(BF16), preferred_element_type=F32)


def _dot_nt(a, b):
    return lax.dot_general(a.astype(BF16), b.astype(BF16), (((1,), (1,)), ((), ())), preferred_element_type=F32)


def _dot_f32(a, b):
    return jnp.dot(a, b, precision=lax.Precision.HIGHEST, preferred_element_type=F32)


def _tri_apply_many(ns, xs, size):
    steps = int(math.log2(size))
    ms = list(ns)
    xs = list(xs)
    for s in range(steps):
        prods = [_dot(m, x) for m, x in zip(ms, xs)]
        if s + 1 < steps:
            ms = [_dot(m, m) for m in ms]
        xs = [x + p for x, p in zip(xs, prods)]
    return xs


def _dwconv_silu(x, w, b):
    n = x.shape[0]
    half = CONV_K // 2

    def taps(xx, masked):
        m = xx.shape[0]
        acc = xx * w[half:half + 1, :] + b
        for k in range(-half, half + 1):
            if k == 0:
                continue
            xr = pltpu.roll(xx, (-k) % m, axis=0)
            if masked:
                row = lax.broadcasted_iota(jnp.int32, xx.shape, 0)
                xr = jnp.where((row + k >= 0) & (row + k < m), xr, 0.0)
            acc = acc + xr * w[k + half:k + half + 1, :]
        return acc

    edge = 8
    head = taps(x[:2 * edge, :], True)[:edge, :]
    tail = taps(x[n - 2 * edge:, :], True)[edge:, :]
    acc = jnp.concatenate([head, taps(x, False)[edge:n - edge, :], tail], axis=0)
    return _silu(acc)


def _seg_sum2(x):
    lane = lax.broadcasted_iota(jnp.int32, x.shape, 1)
    lo = lane < 64
    s_lo = jnp.sum(jnp.where(lo, x, 0.0), axis=1, keepdims=True)
    s_hi = jnp.sum(jnp.where(lo, 0.0, x), axis=1, keepdims=True)
    return jnp.where(lo, s_lo, s_hi)


def _tri_masks(n, forward):
    li = lax.broadcasted_iota(jnp.int32, (n, n), 0)
    si = lax.broadcasted_iota(jnp.int32, (n, n), 1)
    if forward:
        return li >= si, li > si
    return li <= si, li < si


def _ffn_body(x_ref, g_ref, wg_ref, wu_ref, wd_ref, fg_ref, o_ref, xn_ref, acc_ref, *, n_chunks, final):
    x = x_ref[...]
    xn_ref[...] = _rms(x, g_ref[...]).astype(BF16)

    def down(c):
        xb = xn_ref[...]
        gate = jnp.dot(xb, wg_ref[c], preferred_element_type=F32)
        up = jnp.dot(xb, wu_ref[c], preferred_element_type=F32)
        h = (_silu(gate) * up).astype(BF16)
        return jnp.dot(h, wd_ref[c], preferred_element_type=F32)

    def chunk(c, carry):
        acc_ref[...] += down(c)
        return carry

    acc_ref[...] = down(0)
    lax.fori_loop(1, n_chunks, chunk, 0)
    y = x + 0.5 * acc_ref[...]
    if final:
        y = _rms(y, fg_ref[...])
    o_ref[...] = y


def _ffn(x, g, wg3, wu3, wd3, final_g, final):
    t, d = x.shape
    tm = FFN_TOKEN_TILE
    body = functools.partial(_ffn_body, n_chunks=wg3.shape[0], final=final)
    return pl.pallas_call(
        body,
        grid=(t // tm,),
        in_specs=[pl.BlockSpec((tm, d), lambda i: (i, 0)),
                  _resident((1, d)), _resident(wg3.shape), _resident(wu3.shape), _resident(wd3.shape),
                  _resident((1, d))],
        out_specs=pl.BlockSpec((tm, d), lambda i: (i, 0)),
        out_shape=jax.ShapeDtypeStruct((t, d), F32),
        scratch_shapes=[pltpu.VMEM((tm, d), BF16), pltpu.VMEM((tm, d), F32)],
        compiler_params=_cparams("parallel"),
        name="ffn",
    )(x, g, wg3, wu3, wd3, final_g)


def _proj_body(x_ref, g_ref, w_ref, *out_refs, n_main):
    xn = _rms(x_ref[...], g_ref[...]).astype(BF16)
    main_ref = out_refs[0]
    for j in range(0, n_main, PROJ_CHUNK):
        main_ref[:, j:j + PROJ_CHUNK] = jnp.dot(xn, w_ref[:, j:j + PROJ_CHUNK],
                                                preferred_element_type=F32).astype(BF16)
    if len(out_refs) > 1:
        out_refs[1][...] = jnp.dot(xn, w_ref[:, n_main:], preferred_element_type=F32)


def _proj(x, g, w, n_aux):
    t, d = x.shape
    n_main = w.shape[1] - n_aux
    tm = TOKEN_TILE
    out_specs = [pl.BlockSpec((tm, n_main), lambda i: (i, 0))]
    out_shape = [jax.ShapeDtypeStruct((t, n_main), BF16)]
    if n_aux:
        out_specs.append(pl.BlockSpec((tm, n_aux), lambda i: (i, 0)))
        out_shape.append(jax.ShapeDtypeStruct((t, n_aux), F32))
    return pl.pallas_call(
        functools.partial(_proj_body, n_main=n_main),
        grid=(t // tm,),
        in_specs=[pl.BlockSpec((tm, d), lambda i: (i, 0)), _resident((1, d)), _resident(w.shape)],
        out_specs=out_specs,
        out_shape=out_shape,
        compiler_params=_cparams("parallel"),
        name="norm_proj",
    )(x, g, w)


def _ssd_prep_body(p_ref, bias_ref, alog_ref, dt_ref, cs_ref, *, q):
    raw = p_ref[:, 0:LANES]
    dt = _softplus(raw + bias_ref[...])
    dta = dt * (-jnp.exp(alog_ref[...]))
    dt_ref[...] = dt
    incl_f, _ = _tri_masks(q, True)
    incl_b, _ = _tri_masks(q, False)
    tril = incl_f.astype(F32)
    triu = incl_b.astype(F32)
    lane = lax.broadcasted_iota(jnp.int32, (q, LANES), 1)
    n_heads = SSD_N_GROUPS * SSD_HEADS_PER_GROUP
    for c in range(raw.shape[0] // q):
        blk = dta[c * q:(c + 1) * q, :]
        cs_ref[c * q:(c + 1) * q, :] = jnp.where(lane < n_heads, _dot_f32(tril, blk), _dot_f32(triu, blk))


def _ssd_prep(p, bias, alog):
    t = p.shape[0]
    tm = TOKEN_TILE
    blk = pl.BlockSpec((tm, LANES), lambda i: (i, 0))
    return pl.pallas_call(
        functools.partial(_ssd_prep_body, q=SSD_CHUNK),
        grid=(t // tm,),
        in_specs=[pl.BlockSpec((tm, PROJ_AUX), lambda i: (i, 0)),
                  _resident((1, LANES)), _resident((1, LANES))],
        out_specs=[blk, blk],
        out_shape=[jax.ShapeDtypeStruct((t, LANES), F32)] * 2,
        compiler_params=_cparams("parallel"),
        name="ssd_prep",
    )(p, bias, alog)


def _ssd_core_body(xs_ref, bm_ref, cm_ref, wx_ref, wb_ref, wc_ref, bx_ref, bb_ref, bc_ref, dsk_ref,
                   dtr_ref, csr_ref, csc_ref, o_ref, xs_s, b_s, c_s, yf_s, yb_s, st_ref, *, q, n_chunks):
    hd, nh = SSD_HEAD_DIM, SSD_HEADS_PER_GROUP
    xs_s[...] = _dwconv_silu(xs_ref[...].astype(F32), wx_ref[...], bx_ref[...])
    b_s[...] = _dwconv_silu(bm_ref[...].astype(F32), wb_ref[...], bb_ref[...])
    c_s[...] = _dwconv_silu(cm_ref[...].astype(F32), wc_ref[...], bc_ref[...]).astype(BF16)
    st_ref[...] = jnp.zeros_like(st_ref)
    sh = int(math.log2(hd))
    lane_head = jnp.right_shift(lax.broadcasted_iota(jnp.int32, (q, nh * hd), 1), sh)
    lane_head_row = jnp.right_shift(lax.broadcasted_iota(jnp.int32, (1, nh * hd), 1), sh)

    def per_head(vals, lane_map):
        out = vals[nh - 1]
        for h in range(nh - 2, -1, -1):
            out = jnp.where(lane_map == h, vals[h], out)
        return out

    def step(i, carry):
        prep = []
        for d, forward in enumerate((True, False)):
            c = i if forward else n_chunks - 1 - i
            r0 = pl.multiple_of(c * q, q)
            incl, _ = _tri_masks(q, forward)
            cc = c_s[pl.ds(r0, q), :]
            bc = b_s[pl.ds(r0, q), :]
            cb = _dot_nt(cc, bc)
            bct = bc.T
            xc = xs_s[pl.ds(r0, q), :].astype(BF16)
            acols = csc_ref[pl.ds(r0, q), :]
            arows = csr_ref[c]
            dtrows = dtr_ref[c]
            lmats, wmats, eas, tots = [], [], [], []
            for h in range(nh):
                k = d * nh + h
                acol = jnp.broadcast_to(acols[:, k:k + 1], (q, q))
                arow = arows[k:k + 1, :]
                dtr = dtrows[k:k + 1, :]
                tot = arow[:, q - 1:q] if forward else arow[:, 0:1]
                lmats.append((jnp.where(incl, jnp.exp(acol - arow), 0.0) * cb * dtr).astype(BF16))
                wmats.append((bct * (jnp.exp(tot - arow) * dtr)).astype(BF16))
                eas.append(jnp.exp(jnp.concatenate([acol] * (nh * hd // q), axis=1)))
                tots.append(jnp.exp(tot))
            xms = [jnp.where(lane_head == h, xc, jnp.zeros_like(xc)) for h in range(nh)]
            prep.append((r0, cc, lmats, wmats, xms, per_head(eas, lane_head), per_head(tots, lane_head_row)))
        states = [st_ref[d] for d in range(2)]
        y_off = [jnp.dot(p[1], s.astype(BF16), preferred_element_type=F32) for p, s in zip(prep, states)]
        y_diag = [sum(jnp.dot(lm, xm, preferred_element_type=F32) for lm, xm in zip(p[2], p[4])) for p in prep]
        upd = [sum(jnp.dot(wm, xm, preferred_element_type=F32) for wm, xm in zip(p[3], p[4])) for p in prep]
        for d, y_s in enumerate((yf_s, yb_s)):
            y_s[pl.ds(prep[d][0], q), :] = y_diag[d] + y_off[d] * prep[d][5]
            st_ref[d] = states[d] * prep[d][6] + upd[d]
        return carry

    lax.fori_loop(0, n_chunks, step, 0)
    o_ref[...] = yf_s[...] + yb_s[...] + xs_s[...] * dsk_ref[...]


def _ssd_core(p3, conv_w, conv_b, dskip, dt_rows, cs_rows, cs_cols):
    b, l, _ = p3.shape
    q = SSD_CHUNK
    nc = l // q
    gw = SSD_HEADS_PER_GROUP * SSD_HEAD_DIM
    ns = SSD_D_STATE
    x0 = SSD_D_INNER // gw
    b0 = 2 * SSD_D_INNER // ns
    c0 = b0 + SSD_N_GROUPS
    wb0 = SSD_D_INNER // ns
    wc0 = wb0 + SSD_N_GROUPS
    rows_spec = pl.BlockSpec((None, None, nc, 8, q), lambda i, g: (i, g, 0, 0, 0))
    return pl.pallas_call(
        functools.partial(_ssd_core_body, q=q, n_chunks=nc),
        grid=(b, SSD_N_GROUPS),
        in_specs=[pl.BlockSpec((None, l, gw), lambda i, g: (i, 0, x0 + g)),
                  pl.BlockSpec((None, l, ns), lambda i, g: (i, 0, b0 + g)),
                  pl.BlockSpec((None, l, ns), lambda i, g: (i, 0, c0 + g)),
                  pl.BlockSpec((CONV_K, gw), lambda i, g: (0, g)),
                  pl.BlockSpec((CONV_K, ns), lambda i, g: (0, wb0 + g)),
                  pl.BlockSpec((CONV_K, ns), lambda i, g: (0, wc0 + g)),
                  pl.BlockSpec((1, gw), lambda i, g: (0, g)),
                  pl.BlockSpec((1, ns), lambda i, g: (0, wb0 + g)),
                  pl.BlockSpec((1, ns), lambda i, g: (0, wc0 + g)),
                  pl.BlockSpec((1, gw), lambda i, g: (0, g)),
                  rows_spec, rows_spec,
                  pl.BlockSpec((None, None, l, 8), lambda i, g: (i, g, 0, 0))],
        out_specs=pl.BlockSpec((None, l, gw), lambda i, g: (i, 0, g)),
        out_shape=jax.ShapeDtypeStruct((b, l, SSD_D_INNER), F32),
        scratch_shapes=[pltpu.VMEM((l, gw), F32), pltpu.VMEM((l, ns), F32), pltpu.VMEM((l, ns), BF16),
                        pltpu.VMEM((l, gw), F32), pltpu.VMEM((l, gw), F32), pltpu.VMEM((2, ns, gw), F32)],
        compiler_params=_cparams("parallel", "parallel"),
        name="ssd_core",
    )(p3, p3, p3, conv_w, conv_w, conv_w, conv_b, conv_b, conv_b, dskip, dt_rows, cs_rows, cs_cols)


def _ssd_out_body(y_ref, z_ref, nw_ref, w_ref, x_ref, o_ref):
    yz = y_ref[...] * _silu(z_ref[...].astype(F32))
    o_ref[...] = x_ref[...] + jnp.dot(_rms(yz, nw_ref[...]).astype(BF16), w_ref[...], preferred_element_type=F32)


def _ssd_out(y, p, norm_w, w_out, x):
    t, d = x.shape
    tm = TOKEN_TILE
    di = SSD_D_INNER
    return pl.pallas_call(
        _ssd_out_body,
        grid=(t // tm,),
        in_specs=[pl.BlockSpec((tm, di), lambda i: (i, 0)),
                  pl.BlockSpec((tm, di), lambda i: (i, 0)),
                  _resident((1, di)), _resident(w_out.shape),
                  pl.BlockSpec((tm, d), lambda i: (i, 0))],
        out_specs=pl.BlockSpec((tm, d), lambda i: (i, 0)),
        out_shape=jax.ShapeDtypeStruct((t, d), F32),
        compiler_params=_cparams("parallel"),
        name="ssd_out",
    )(y, p, norm_w, w_out, x)


def _group_rows_cols(a, b, l, nc, q):
    g = a.shape[2]
    cols = jnp.transpose(a, (0, 2, 1, 3))
    rows = jnp.transpose(a.reshape(b, nc, q, g, 8), (0, 3, 1, 4, 2))
    return rows, cols


def _ssd_mixer(x, b, l, g_norm, w_in, conv_w, conv_b, a_log, dt_bias, d_skip, norm_w, w_out):
    t = b * l
    q = SSD_CHUNK
    nc = l // q
    p, aux = _proj(x, g_norm, w_in, PROJ_AUX)
    dt, cs = _ssd_prep(aux, dt_bias, a_log)
    ng, nh = SSD_N_GROUPS, SSD_HEADS_PER_GROUP

    def regroup(a):
        a = a[:, :2 * ng * nh].reshape(b, l, 2, ng, nh)
        return jnp.transpose(a, (0, 1, 3, 2, 4)).reshape(b, l, ng, 2 * nh)

    dt_rows, _ = _group_rows_cols(regroup(dt), b, l, nc, q)
    cs_rows, cs_cols = _group_rows_cols(regroup(cs), b, l, nc, q)
    y = _ssd_core(p.reshape(b, l, -1), conv_w, conv_b, d_skip, dt_rows, cs_rows, cs_cols)
    return _ssd_out(y.reshape(t, SSD_D_INNER), p, norm_w, w_out, x)


def _gdn_prep_body(p_ref, bias_ref, alog_ref, o_ref, *, cs):
    raw = p_ref[:, 0:LANES]
    beta = _sigmoid(raw)
    g = -jnp.exp(alog_ref[...]) * _softplus(raw + bias_ref[...])
    n = LANES
    li = lax.broadcasted_iota(jnp.int32, (n, n), 0)
    si = lax.broadcasted_iota(jnp.int32, (n, n), 1)
    sh = int(math.log2(cs))
    same = jnp.right_shift(li, sh) == jnp.right_shift(si, sh)
    tril = (same & (li >= si)).astype(F32)
    triu = (same & (li <= si)).astype(F32)
    lane = lax.broadcasted_iota(jnp.int32, (n, LANES), 1)
    hv = GDN_N_V_HEADS
    for c in range(raw.shape[0] // n):
        blk = g[c * n:(c + 1) * n, :]
        o_ref[c * n:(c + 1) * n, :] = jnp.where(
            lane < hv, beta[c * n:(c + 1) * n, :],
            jnp.where(lane < 2 * hv, _dot_f32(tril, blk), _dot_f32(triu, blk)))


def _gdn_prep(p, bias, alog):
    t = p.shape[0]
    tm = TOKEN_TILE
    return pl.pallas_call(
        functools.partial(_gdn_prep_body, cs=GDN_CHUNK),
        grid=(t // tm,),
        in_specs=[pl.BlockSpec((tm, PROJ_AUX), lambda i: (i, 0)),
                  _resident((1, LANES)), _resident((1, LANES))],
        out_specs=pl.BlockSpec((tm, LANES), lambda i: (i, 0)),
        out_shape=jax.ShapeDtypeStruct((t, LANES), F32),
        compiler_params=_cparams("parallel"),
        name="gdn_prep",
    )(p, bias, alog)


def _gdn_core_body(q_ref, k_ref, v_ref, wq_ref, wk_ref, wv_ref, bq_ref, bk_ref, bv_ref, grow_ref, cols_ref,
                   o_ref, q_s, k_s, v_s, mg_s, c_s, d_s, yf_s, yb_s, st_ref, *, cs, n_chunks):
    hd = GDN_HEAD
    qn = _dwconv_silu(q_ref[...].astype(F32), wq_ref[...], bq_ref[...])
    qn = qn * lax.rsqrt(jnp.sum(qn * qn, axis=-1, keepdims=True) + 1e-6)
    q_s[...] = qn * (hd ** -0.5)
    kn = _dwconv_silu(k_ref[...].astype(F32), wk_ref[...], bk_ref[...])
    k_s[...] = kn * lax.rsqrt(jnp.sum(kn * kn, axis=-1, keepdims=True) + 1e-6)
    v_s[...] = _dwconv_silu(v_ref[...].astype(F32), wv_ref[...], bv_ref[...])
    st_ref[...] = jnp.zeros_like(st_ref)

    lane_blk = jnp.right_shift(lax.broadcasted_iota(jnp.int32, (cs, 4 * cs), 1), int(math.log2(cs)))
    t_row = lax.broadcasted_iota(jnp.int32, (cs, 4 * cs), 0)
    t_col = jnp.bitwise_and(lax.broadcasted_iota(jnp.int32, (cs, 4 * cs), 1), cs - 1)
    eye_packed = jnp.where(t_row == t_col, 1.0, 0.0)
    fwd_blk = lane_blk < 2
    incl_p = (fwd_blk & (t_row >= t_col)) | (jnp.logical_not(fwd_blk) & (t_row <= t_col))
    strict_p = (fwd_blk & (t_row > t_col)) | (jnp.logical_not(fwd_blk) & (t_row < t_col))
    head0 = lax.broadcasted_iota(jnp.int32, (cs, 2 * hd), 1) < hd
    head0_row = lax.broadcasted_iota(jnp.int32, (1, 2 * hd), 1) < hd
    half_row = lax.broadcasted_iota(jnp.int32, (1, 2 * cs), 1) < cs
    half3 = lax.broadcasted_iota(jnp.int32, (3 * cs, 2 * cs), 1) < cs

    def block_diag(z):
        return jnp.concatenate([jnp.where(lane_blk == r, z, 0.0) for r in range(4)], axis=0).astype(BF16)

    def pick(cols, idx):
        out = jnp.where(lane_blk == 2, cols[:, idx[2]:idx[2] + 1], cols[:, idx[3]:idx[3] + 1])
        out = jnp.where(lane_blk == 1, cols[:, idx[1]:idx[1] + 1], out)
        return jnp.where(lane_blk == 0, cols[:, idx[0]:idx[0] + 1], out)

    def chunk_end(g, forward):
        return (g[:, cs - 1:cs], g[:, 2 * cs - 1:2 * cs]) if forward else (g[:, 0:1], g[:, cs:cs + 1])

    def factor(i, carry):
        packs = []
        for j in range(FACTOR_CHUNKS):
            c = i * FACTOR_CHUNKS + j
            r0 = pl.multiple_of(c * cs, cs)
            c3 = pl.multiple_of(c * 3 * cs, cs)
            c4 = pl.multiple_of(c * 4 * cs, 4 * cs)
            qc = q_s[pl.ds(r0, cs), :]
            kc = k_s[pl.ds(r0, cs), :]
            vc = v_s[pl.ds(r0, cs), :]
            k4 = jnp.concatenate([kc] * 4, axis=0)
            packs.append((r0, c3, c4, qc, kc, vc, cols_ref[pl.ds(r0, cs), :], grow_ref[c],
                          _dot_nt(kc, k4), _dot_nt(qc, k4)))
        n_p, rhs_all, lhs_all, qd_all = [], [], [], []
        for r0, c3, c4, qc, kc, vc, cols, grow_p, kk4, qk4 in packs:
            dec_p = jnp.where(incl_p, jnp.exp(pick(cols, (2, 3, 4, 5)) - grow_p), 0.0)
            n_p.append(jnp.where(strict_p, -(pick(cols, (0, 1, 0, 1)) * kk4 * dec_p), 0.0))
            qkm = jnp.where(incl_p, qk4 * dec_p, 0.0)
            kt2 = jnp.concatenate([kc, kc], axis=0).T
            rhs, lhs, qds = [], [], []
            for d, forward in enumerate((True, False)):
                g = grow_p[:, d * 2 * cs:(d + 1) * 2 * cs]
                gl0, gl1 = chunk_end(g, forward)
                pair = jnp.concatenate([kt2 * jnp.exp(jnp.where(half_row, gl0, gl1) - g),
                                        qkm[:, d * 2 * cs:(d + 1) * 2 * cs]], axis=0)
                for e in range(2):
                    bcol = cols[:, e:e + 1]
                    eg = jnp.exp(cols[:, 2 + 2 * d + e:3 + 2 * d + e])
                    rhs.append(jnp.concatenate([vc[:, e * hd:(e + 1) * hd] * bcol, kc * (bcol * eg)],
                                               axis=1).astype(BF16))
                    lhs.append(jnp.where(half3 if e == 0 else jnp.logical_not(half3), pair, 0.0).astype(BF16))
                    qds.append(qc * eg)
            rhs_all.append(rhs)
            lhs_all.append(lhs)
            qd_all.append(qds)
        m_p = n_p
        inv_p = [eye_packed + m for m in n_p]
        steps = int(math.log2(cs))
        for s in range(1, steps):
            bds = [block_diag(m) for m in m_p]
            if s == 1:
                m_p = [jnp.dot(m.astype(BF16), bd, preferred_element_type=F32) for m, bd in zip(m_p, bds)]
                continue
            res = [jnp.dot(jnp.concatenate([p, m], axis=0).astype(BF16), bd, preferred_element_type=F32)
                   for p, m, bd in zip(inv_p, m_p, bds)]
            inv_p = [p + r[:cs, :] for p, r in zip(inv_p, res)]
            m_p = [r[cs:, :] for r in res]
        bds = [block_diag(m) for m in m_p]
        inv_p = [(p + jnp.dot(p.astype(BF16), bd, preferred_element_type=F32)).astype(BF16)
                 for p, bd in zip(inv_p, bds)]
        zero = jnp.zeros((cs, 2 * hd), BF16)
        sols = [[jnp.dot(p, jnp.concatenate([zero] * r + [rhs[r]] + [zero] * (3 - r), axis=0),
                         preferred_element_type=F32) for r in range(4)] for p, rhs in zip(inv_p, rhs_all)]
        folded = [[jnp.dot(lh[2 * d + e],
                           jnp.concatenate([sol[2 * d], sol[2 * d + 1]], axis=0).astype(BF16),
                           preferred_element_type=F32) for d in range(2) for e in range(2)]
                  for lh, sol in zip(lhs_all, sols)]
        for (r0, c3, c4, *_), fold, qds in zip(packs, folded, qd_all):
            c6 = pl.multiple_of(c3 * 2, 2 * cs)
            c2 = pl.multiple_of(r0 * 2, 2 * cs)
            for d in range(2):
                f0, f1 = fold[2 * d], fold[2 * d + 1]
                mg_s[d, pl.ds(c6, 6 * cs), :] = jnp.concatenate(
                    [-f0[:2 * cs, hd:], qds[2 * d] - f0[2 * cs:, hd:],
                     -f1[:2 * cs, hd:], qds[2 * d + 1] - f1[2 * cs:, hd:]], axis=0).astype(BF16)
                c_s[d, pl.ds(c2, 2 * cs), :] = jnp.concatenate([f0[:2 * cs, :hd], f1[:2 * cs, :hd]], axis=1)
                d_s[d, pl.ds(r0, cs), :] = jnp.concatenate([f0[2 * cs:, :hd], f1[2 * cs:, :hd]], axis=1)
        return carry

    head0_k = lax.broadcasted_iota(jnp.int32, (hd, 2 * hd), 1) < hd

    def scan(i, carry):
        cidx = [i, n_chunks - 1 - i]
        states = [st_ref[d] for d in range(2)]
        res = [jnp.dot(mg_s[d, pl.ds(pl.multiple_of(cidx[d] * 6 * cs, 2 * cs), 6 * cs), :], states[d].astype(BF16),
                       preferred_element_type=F32) for d in range(2)]
        for d, (forward, y_s) in enumerate(((True, yf_s), (False, yb_s))):
            c = cidx[d]
            gl0, gl1 = chunk_end(grow_ref[c][:, d * 2 * cs:(d + 1) * 2 * cs], forward)
            r = res[d]
            y_s[pl.ds(pl.multiple_of(c * cs, cs), cs), :] = (
                jnp.where(head0, r[2 * cs:3 * cs, :], r[5 * cs:, :]) + d_s[d, pl.ds(pl.multiple_of(c * cs, cs), cs), :])
            st_ref[d] = (states[d] * jnp.exp(jnp.where(head0_row, gl0, gl1))
                         + jnp.where(head0_k, r[:2 * cs, :], r[3 * cs:5 * cs, :])
                         + c_s[d, pl.ds(pl.multiple_of(c * 2 * cs, 2 * cs), 2 * cs), :])
        return carry

    lax.fori_loop(0, n_chunks // FACTOR_CHUNKS, factor, 0)
    lax.fori_loop(0, n_chunks, scan, 0)
    o_ref[...] = yf_s[...] + yb_s[...]


def _gdn_core(p3, conv_w, conv_b, grow4, cols):
    b, l, _ = p3.shape
    cs = GDN_CHUNK
    nc = l // cs
    hd = GDN_HEAD
    vw = 2 * hd
    k0 = GDN_QK_DIM // hd
    v0 = 2 * GDN_QK_DIM // vw
    assert l % (cs * FACTOR_CHUNKS) == 0, (l, cs, FACTOR_CHUNKS)
    return pl.pallas_call(
        functools.partial(_gdn_core_body, cs=cs, n_chunks=nc),
        grid=(b, GDN_N_K_HEADS),
        in_specs=[pl.BlockSpec((None, l, hd), lambda i, j: (i, 0, j)),
                  pl.BlockSpec((None, l, hd), lambda i, j: (i, 0, k0 + j)),
                  pl.BlockSpec((None, l, vw), lambda i, j: (i, 0, v0 + j)),
                  pl.BlockSpec((CONV_K, hd), lambda i, j: (0, j)),
                  pl.BlockSpec((CONV_K, hd), lambda i, j: (0, k0 + j)),
                  pl.BlockSpec((CONV_K, vw), lambda i, j: (0, v0 + j)),
                  pl.BlockSpec((1, hd), lambda i, j: (0, j)),
                  pl.BlockSpec((1, hd), lambda i, j: (0, k0 + j)),
                  pl.BlockSpec((1, vw), lambda i, j: (0, v0 + j)),
                  pl.BlockSpec((None, None, nc, 1, 4 * cs), lambda i, j: (i, j, 0, 0, 0)),
                  pl.BlockSpec((None, None, l, 8), lambda i, j: (i, j, 0, 0))],
        out_specs=pl.BlockSpec((None, l, vw), lambda i, j: (i, 0, j)),
        out_shape=jax.ShapeDtypeStruct((b, l, GDN_V_DIM), F32),
        scratch_shapes=[pltpu.VMEM((l, hd), F32), pltpu.VMEM((l, hd), F32), pltpu.VMEM((l, vw), F32),
                        pltpu.VMEM((2, 6 * l, hd), BF16),
                        pltpu.VMEM((2, 2 * l, vw), F32),
                        pltpu.VMEM((2, l, vw), F32),
                        pltpu.VMEM((l, vw), F32), pltpu.VMEM((l, vw), F32), pltpu.VMEM((2, hd, vw), F32)],
        compiler_params=_cparams("parallel", "parallel"),
        name="gdn_core",
    )(p3, p3, p3, conv_w, conv_w, conv_w, conv_b, conv_b, conv_b, grow4, cols)


def _gdn_out_body(o_ref, z_ref, nw_ref, w_ref, x_ref, out_ref, h_s):
    hd = GDN_HEAD
    nw = nw_ref[...]
    for e in range(GDN_N_V_HEADS):
        oe = o_ref[:, e * hd:(e + 1) * hd]
        ze = z_ref[:, e * hd:(e + 1) * hd].astype(F32)
        h_s[:, e * hd:(e + 1) * hd] = (_rms(oe, nw) * _silu(ze)).astype(BF16)
    out_ref[...] = x_ref[...] + jnp.dot(h_s[...], w_ref[...], preferred_element_type=F32)


def _gdn_out(o, p, norm_w, w_out, x):
    t, d = x.shape
    tm = TOKEN_TILE
    dv = GDN_V_DIM
    return pl.pallas_call(
        _gdn_out_body,
        grid=(t // tm,),
        in_specs=[pl.BlockSpec((tm, dv), lambda i: (i, 0)),
                  pl.BlockSpec((tm, dv), lambda i: (i, 2 * GDN_QK_DIM // dv + 1)),
                  _resident((1, GDN_HEAD)), _resident(w_out.shape),
                  pl.BlockSpec((tm, d), lambda i: (i, 0))],
        out_specs=pl.BlockSpec((tm, d), lambda i: (i, 0)),
        out_shape=jax.ShapeDtypeStruct((t, d), F32),
        scratch_shapes=[pltpu.VMEM((tm, dv), BF16)],
        compiler_params=_cparams("parallel"),
        name="gdn_out",
    )(o, p, norm_w, w_out, x)


def _gdn_mixer(x, b, l, g_norm, w_in, conv_w, conv_b, a_log, dt_bias, norm_w, w_out):
    t = b * l
    cs = GDN_CHUNK
    nc = l // cs
    p, aux = _proj(x, g_norm, w_in, PROJ_AUX)
    gt = _gdn_prep(aux, dt_bias, a_log).reshape(b, l, LANES)
    hk, hv = GDN_N_K_HEADS, GDN_N_V_HEADS
    parts = [gt[..., i * hv:(i + 1) * hv].reshape(b, l, hk, 2) for i in range(3)]
    a = jnp.concatenate(parts + [jnp.zeros((b, l, hk, 2), F32)], axis=-1)
    cols = jnp.transpose(a, (0, 2, 1, 3))
    grow4 = jnp.transpose(a[..., 2:6].reshape(b, nc, cs, hk, 4), (0, 3, 1, 4, 2)).reshape(b, hk, nc, 1, 4 * cs)
    o = _gdn_core(p.reshape(b, l, -1), conv_w, conv_b, grow4, cols)
    return _gdn_out(o.reshape(t, GDN_V_DIM), p, norm_w, w_out, x)


def _att_core_body(q_ref, k_ref, v_ref, cos_ref, sin_ref, sink_ref, o_ref, q_s, k_s, v_s, band_s,
                   *, win, n_blocks, seq):
    dh = ATT_HEAD_DIM
    cos = cos_ref[...]
    sin = sin_ref[...]

    def rope(x):
        w = x.shape[1]
        lane = lax.broadcasted_iota(jnp.int32, x.shape, 1)
        first = jnp.bitwise_and(lane, dh - 1) < dh // 2
        partner = jnp.where(first, pltpu.roll(x, w - dh // 2, axis=1), pltpu.roll(x, dh // 2, axis=1))
        reps = w // LANES
        c = jnp.concatenate([cos] * reps, axis=1) if reps > 1 else cos
        s = jnp.concatenate([sin] * reps, axis=1) if reps > 1 else sin
        return x * c + partner * s

    n_heads = q_ref.shape[1] // dh
    rep = ATT_N_HEADS // ATT_N_KV_HEADS
    groups = range(n_heads // rep)
    qr = rope(q_ref[...].astype(F32)) * (dh ** -0.5)
    for h in range(n_heads):
        q_s[h] = qr[:, h * dh:(h + 1) * dh].astype(BF16)
    kr = rope(k_ref[...].astype(F32))
    vv = v_ref[...]
    ones = jnp.ones((seq, dh), BF16)
    for g in groups:
        for ref in (k_s, v_s):
            ref[g, pl.ds(0, win), :] = jnp.zeros((win, ref.shape[2]), BF16)
            ref[g, pl.ds(win + seq, win), :] = jnp.zeros((win, ref.shape[2]), BF16)
        k_s[g, pl.ds(win, seq), :] = kr[:, g * dh:(g + 1) * dh].astype(BF16)
        v_s[g, pl.ds(win, seq), :] = jnp.concatenate([vv[:, g * dh:(g + 1) * dh], ones], axis=1)
    qi = lax.broadcasted_iota(jnp.int32, (win, 3 * win), 0)
    ki = lax.broadcasted_iota(jnp.int32, (win, 3 * win), 1)
    band_s[...] = jnp.where(jnp.abs(ki - win - qi) <= win, 0.0, -jnp.inf)
    sinks = sink_ref[...]

    def block(n, carry):
        r0 = pl.multiple_of(n * win, win)
        kpos = r0 - win + lax.broadcasted_iota(jnp.int32, (1, 3 * win), 1)
        bias = band_s[...] + jnp.where((kpos >= 0) & (kpos < seq), 0.0, -jnp.inf)
        scores = []
        for g in groups:
            qg = jnp.concatenate([q_s[g * rep + r, pl.ds(r0, win), :] for r in range(rep)], axis=0)
            scores.append(_dot_nt(qg, k_s[g, pl.ds(r0, 3 * win), :]))
        exps, maxes = [], []
        for g in groups:
            es = []
            for r in range(rep):
                s = scores[g][r * win:(r + 1) * win, :] + bias
                sink = sinks[:, g * rep + r:g * rep + r + 1]
                m = jnp.maximum(jnp.max(s, axis=-1, keepdims=True), sink)
                es.append(jnp.exp(s - m).astype(BF16))
                maxes.append(m)
            exps.append(jnp.concatenate(es, axis=0))
        nds = [jnp.dot(exps[g], v_s[g, pl.ds(r0, 3 * win), :], preferred_element_type=F32) for g in groups]
        outs = []
        for g in groups:
            for r in range(rep):
                h = g * rep + r
                nd = nds[g][r * win:(r + 1) * win, :]
                outs.append(nd[:, :dh] / (nd[:, dh:] + jnp.exp(sinks[:, h:h + 1] - maxes[h])))
        o_ref[pl.ds(r0, win), :] = jnp.concatenate(outs, axis=1).astype(o_ref.dtype)
        return carry

    lax.fori_loop(0, n_blocks, block, 0)


def _att_core(p3, cos2, sin2, sinks3):
    b, l, _ = p3.shape
    win = ATT_WINDOW
    dh = ATT_HEAD_DIM
    steps = ATT_N_KV_HEADS * dh // LANES
    qw = ATT_N_HEADS * dh // steps
    k0 = ATT_N_HEADS * dh // LANES
    v0 = k0 + steps
    return pl.pallas_call(
        functools.partial(_att_core_body, win=win, n_blocks=l // win, seq=l),
        grid=(b, steps),
        in_specs=[pl.BlockSpec((None, l, qw), lambda i, j: (i, 0, j)),
                  pl.BlockSpec((None, l, LANES), lambda i, j: (i, 0, k0 + j)),
                  pl.BlockSpec((None, l, LANES), lambda i, j: (i, 0, v0 + j)),
                  _resident((l, LANES)), _resident((l, LANES)),
                  pl.BlockSpec((None, 1, qw // dh), lambda i, j: (j, 0, 0))],
        out_specs=pl.BlockSpec((None, l, qw), lambda i, j: (i, 0, j)),
        out_shape=jax.ShapeDtypeStruct((b, l, ATT_N_HEADS * dh), BF16),
        scratch_shapes=[pltpu.VMEM((qw // dh, l, dh), BF16),
                        pltpu.VMEM((LANES // dh, l + 2 * win, dh), BF16),
                        pltpu.VMEM((LANES // dh, l + 2 * win, 2 * dh), BF16),
                        pltpu.VMEM((win, 3 * win), F32)],
        compiler_params=_cparams("parallel", "parallel"),
        name="att_core",
    )(p3, p3, p3, cos2, sin2, sinks3)


def _res_proj_body(a_ref, w_ref, x_ref, o_ref):
    o_ref[...] = x_ref[...] + jnp.dot(a_ref[...], w_ref[...], preferred_element_type=F32)


def _res_proj(a, w, x):
    t, d = x.shape
    k = a.shape[1]
    tm = TOKEN_TILE
    return pl.pallas_call(
        _res_proj_body,
        grid=(t // tm,),
        in_specs=[pl.BlockSpec((tm, k), lambda i: (i, 0)), _resident(w.shape),
                  pl.BlockSpec((tm, d), lambda i: (i, 0))],
        out_specs=pl.BlockSpec((tm, d), lambda i: (i, 0)),
        out_shape=jax.ShapeDtypeStruct((t, d), F32),
        compiler_params=_cparams("parallel"),
        name="res_proj",
    )(a, w, x)


def _att_mixer(x, b, l, g_norm, w_qkv, cos2, sin2, sinks3, w_out):
    t = b * l
    (p,) = _proj(x, g_norm, w_qkv, 0)
    o = _att_core(p.reshape(b, l, -1), cos2, sin2, sinks3)
    return _res_proj(o.reshape(t, -1), w_out, x)


def _rwkv_prep_body(x_ref, xp_ref, xn_ref, g_ref, mu_ref, wr_ref, wk_ref, wv_ref, w1_ref, w2_ref, w0_ref,
                    a0_ref, a1_ref, a2_ref, g1_ref, g2_ref, kk_ref, ka_ref,
                    r_o, k_o, v_o, kk_o, a_o, lw0_o, lw1_o, gate_o, *, n_tiles):
    i = pl.program_id(1)
    g = g_ref[...]
    u = _rms(x_ref[...], g)
    tl = u.shape[0]
    prev_row = jnp.where(i > 0, _rms(xp_ref[...], g)[7:8, :], 0.0)
    next_row = jnp.where(i < n_tiles - 1, _rms(xn_ref[...], g)[0:1, :], 0.0)
    row = lax.broadcasted_iota(jnp.int32, u.shape, 0)
    u_prev = jnp.where(row == 0, prev_row, pltpu.roll(u, 1, axis=0))
    u_next = jnp.where(row == tl - 1, next_row, pltpu.roll(u, tl - 1, axis=0))
    xx = 0.5 * (u_prev + u_next) - u
    mu = mu_ref[...]

    def mix(s):
        return (u + xx * mu[s:s + 1, :]).astype(BF16)

    r_o[...] = jnp.dot(mix(0), wr_ref[...], preferred_element_type=F32)
    k = jnp.dot(mix(1), wk_ref[...], preferred_element_type=F32)
    v_o[...] = jnp.dot(mix(2), wv_ref[...], preferred_element_type=F32)
    wl = jnp.tanh(jnp.dot(mix(3), w1_ref[...], preferred_element_type=F32))
    lora = w2_ref.shape[1]
    w0 = w0_ref[...]
    for d, out in enumerate((lw0_o, lw1_o)):
        z = w0[d:d + 1, :] + _dot(wl[:, d * lora:(d + 1) * lora], w2_ref[d])
        out[...] = -jnp.exp(-_softplus(-z) - 0.5)
    a = _sigmoid(a0_ref[...] + _dot(jnp.dot(mix(4), a1_ref[...], preferred_element_type=F32), a2_ref[...]))
    gate_o[...] = _dot(_sigmoid(jnp.dot(mix(5), g1_ref[...], preferred_element_type=F32)), g2_ref[...])
    a_o[...] = a
    kk_o[...] = k * kk_ref[...]
    k_o[...] = k * (1.0 + (a - 1.0) * ka_ref[...])


def _rwkv_prep(x3, g, mu, wr, wk, wv, w1, w2, w0, a0, a1, a2, g1, g2, k_k, k_a):
    b, l, d = x3.shape
    tl = 256
    nt = l // tl
    x4 = x3.reshape(b, l // 8, 8, d)
    tile = pl.BlockSpec((None, tl, d), lambda i, j: (i, j, 0))
    consts = [g, mu, wr, wk, wv, w1, w2, w0, a0, a1, a2, g1, g2, k_k, k_a]
    return pl.pallas_call(
        functools.partial(_rwkv_prep_body, n_tiles=nt),
        grid=(b, nt),
        in_specs=[tile,
                  pl.BlockSpec((None, None, 8, d), lambda i, j: (i, jnp.maximum(j * (tl // 8) - 1, 0), 0, 0)),
                  pl.BlockSpec((None, None, 8, d),
                               lambda i, j: (i, jnp.minimum((j + 1) * (tl // 8), l // 8 - 1), 0, 0))]
                 + [_resident(c.shape) for c in consts],
        out_specs=[tile] * 8,
        out_shape=[jax.ShapeDtypeStruct((b, l, d), F32)] * 8,
        compiler_params=_cparams("parallel", "parallel"),
        name="rwkv_prep",
    )(x3, x4, x4, *consts)


def _rwkv_core_body(r_ref, k_ref, v_ref, kk_ref, a_ref, lw0_ref, lw1_ref, rk_ref, lnw_ref, lnb_ref, o_ref,
                    kk_s, kb_s, mg_s, cd_s, pe_s, yf_s, yb_s, st_ref, *, cs, n_chunks):
    n = RWKV_HEAD
    kk = kk_ref[...]
    kk = kk * lax.rsqrt(_seg_sum2(kk * kk) + 1e-6)
    kk_s[...] = kk
    kb_s[...] = kk * a_ref[...]
    st_ref[...] = jnp.zeros_like(st_ref)

    lane_blk = jnp.right_shift(lax.broadcasted_iota(jnp.int32, (cs, 4 * cs), 1), int(math.log2(cs)))
    t_row = lax.broadcasted_iota(jnp.int32, (cs, 4 * cs), 0)
    t_col = jnp.bitwise_and(lax.broadcasted_iota(jnp.int32, (cs, 4 * cs), 1), cs - 1)
    eye_packed = jnp.where(t_row == t_col, 1.0, 0.0)
    head0 = lax.broadcasted_iota(jnp.int32, (cs, LANES), 1) < n
    head0_rows = lax.broadcasted_iota(jnp.int32, (LANES, cs), 0) < n

    def block_diag(z):
        return jnp.concatenate([jnp.where(lane_blk == r, z, 0.0) for r in range(4)], axis=0).astype(BF16)

    def split_heads_rows(z):
        return jnp.concatenate([jnp.where(head0, z, 0.0), jnp.where(head0, 0.0, z)], axis=0)

    def factor(i, carry):
        span = FACTOR_CHUNKS * cs
        base = pl.multiple_of(i * span, span)
        rows = pl.ds(base, span)
        rc2, kc2, vc2, kkc2, kbc2 = r_ref[rows, :], k_ref[rows, :], v_ref[rows, :], kk_s[rows, :], kb_s[rows, :]
        li = lax.broadcasted_iota(jnp.int32, (2 * cs, 2 * cs), 0)
        si = lax.broadcasted_iota(jnp.int32, (2 * cs, 2 * cs), 1)
        sh = int(math.log2(cs))
        same = jnp.right_shift(li, sh) == jnp.right_shift(si, sh)
        lws, cums = [], []
        for forward, lw_ref in ((True, lw0_ref), (False, lw1_ref)):
            lws.append(lw_ref[rows, :])
            tri = (same & ((li >= si) if forward else (li <= si))).astype(F32)
            cums.append(jnp.concatenate([_dot_f32(tri, lws[-1][p * 2 * cs:(p + 1) * 2 * cs, :])
                                         for p in range(FACTOR_CHUNKS // 2)], axis=0))
        units, lhss, rhss = [], [], []
        for j in range(FACTOR_CHUNKS):
            rs = slice(j * cs, (j + 1) * cs)
            c = i * FACTOR_CHUNKS + j
            r0 = pl.multiple_of(c * cs, cs)
            c2 = pl.multiple_of(c * 2 * cs, 2 * cs)
            c3 = pl.multiple_of(c * 3 * cs, cs)
            for d, forward in enumerate((True, False)):
                lw, cum = lws[d][rs, :], cums[d][rs, :]
                tot = cum[cs - 1:cs, :] if forward else cum[0:1, :]
                e_neg = jnp.exp(-cum)
                e_end = jnp.exp(tot - cum)
                a_hat = -kkc2[rs, :] * jnp.exp(cum - lw)
                r_hat = rc2[rs, :] * jnp.exp(cum)
                b_end_t = (kbc2[rs, :] * e_end).T
                k_end_t = (kc2[rs, :] * e_end).T
                upd_lhs = jnp.concatenate(
                    [jnp.where(head0_rows, b_end_t, 0.0), jnp.where(head0_rows, 0.0, b_end_t),
                     jnp.where(head0_rows, k_end_t, 0.0), jnp.where(head0_rows, 0.0, k_end_t)], axis=1)
                p_col = jnp.broadcast_to(jnp.exp(tot), (8, LANES)).T[:, 0:1]
                pe_s[d, pl.ds(c2, 2 * cs), :] = jnp.broadcast_to(p_col, (LANES, LANES))
                lhss.append(jnp.concatenate([a_hat, r_hat], axis=0))
                rhss.append(jnp.concatenate([split_heads_rows(kbc2[rs, :] * e_neg),
                                             split_heads_rows(kc2[rs, :] * e_neg)], axis=0))
                units.append((j, d, forward, r0, c2, c3, a_hat, r_hat, upd_lhs))
        grams = [_dot_nt(lh, rh) for lh, rh in zip(lhss, rhss)]
        tops = []
        lhs_all = []
        for (j, d, forward, r0, c2, c3, _, _, upd_lhs), gram in zip(units, grams):
            strict = (t_row > t_col) if forward else (t_row < t_col)
            incl = (t_row >= t_col) if forward else (t_row <= t_col)
            tops.append(jnp.where(strict, gram[:cs, :], 0.0))
            lhs_all.append(jnp.concatenate([jnp.where(incl, gram[cs:, :], 0.0), upd_lhs], axis=0).astype(BF16))
        n_p = [jnp.concatenate([tops[2 * j][:, :LANES], tops[2 * j + 1][:, :LANES]], axis=1)
               for j in range(FACTOR_CHUNKS)]
        ak_p = [jnp.concatenate([tops[2 * j][:, LANES:], tops[2 * j + 1][:, LANES:]], axis=1)
                for j in range(FACTOR_CHUNKS)]
        a_p = [jnp.concatenate([units[2 * j][6], units[2 * j + 1][6]], axis=1) for j in range(FACTOR_CHUNKS)]
        v_bd = [block_diag(jnp.concatenate([vc2[j * cs:(j + 1) * cs, :]] * 2, axis=1))
                for j in range(FACTOR_CHUNKS)]
        av_p = [jnp.dot(ak.astype(BF16), vb, preferred_element_type=F32) for ak, vb in zip(ak_p, v_bd)]
        m_p = n_p
        inv_p = [eye_packed + m for m in n_p]
        steps = int(math.log2(cs))
        for s in range(1, steps):
            bds = [block_diag(m) for m in m_p]
            if s == 1:
                m_p = [jnp.dot(m.astype(BF16), bd, preferred_element_type=F32) for m, bd in zip(m_p, bds)]
                continue
            res = [jnp.dot(jnp.concatenate([p, m], axis=0).astype(BF16), bd, preferred_element_type=F32)
                   for p, m, bd in zip(inv_p, m_p, bds)]
            inv_p = [p + r[:cs, :] for p, r in zip(inv_p, res)]
            m_p = [r[cs:, :] for r in res]
        bds = [block_diag(m) for m in m_p]
        inv_p = [p + jnp.dot(p.astype(BF16), bd, preferred_element_type=F32) for p, bd in zip(inv_p, bds)]
        sols = [jnp.dot(p.astype(BF16), jnp.concatenate([block_diag(a), block_diag(av)], axis=1),
                        preferred_element_type=F32) for p, a, av in zip(inv_p, a_p, av_p)]
        w_rows, uv_rows = [], []
        for u, (j, d, *_rest) in enumerate(units):
            w_rows.append(split_heads_rows(sols[j][:, d * LANES:(d + 1) * LANES]).astype(BF16))
            uv_rows.append(jnp.concatenate([split_heads_rows(sols[j][:, (2 + d) * LANES:(3 + d) * LANES]),
                                            split_heads_rows(vc2[j * cs:(j + 1) * cs, :])], axis=0).astype(BF16))
        by_w = [jnp.dot(lh[:, :2 * cs], w, preferred_element_type=F32) for lh, w in zip(lhs_all, w_rows)]
        by_u = [jnp.dot(lh, uv, preferred_element_type=F32) for lh, uv in zip(lhs_all, uv_rows)]
        for (j, d, _, r0, c2, c3, _, r_hat, _), bw, bu in zip(units, by_w, by_u):
            mg_s[d, pl.ds(c3, 3 * cs), :] = jnp.concatenate([bw[cs:, :], r_hat + bw[:cs, :]], axis=0).astype(BF16)
            cd_s[d, pl.ds(c3, 3 * cs), :] = jnp.concatenate([bu[cs:, :], bu[:cs, :]], axis=0)
        return carry

    def scan(i, carry):
        cidx = [i, n_chunks - 1 - i]
        states = [st_ref[d] for d in range(2)]
        res = [jnp.dot(mg_s[d, pl.ds(pl.multiple_of(cidx[d] * 3 * cs, cs), 3 * cs), :], states[d].astype(BF16),
                       preferred_element_type=F32) for d in range(2)]
        for d, y_s in enumerate((yf_s, yb_s)):
            cd = cd_s[d, pl.ds(pl.multiple_of(cidx[d] * 3 * cs, cs), 3 * cs), :]
            decay = pe_s[d, pl.ds(pl.multiple_of(cidx[d] * 2 * cs, 2 * cs), 2 * cs), :]
            y_s[pl.ds(pl.multiple_of(cidx[d] * cs, cs), cs), :] = res[d][2 * cs:, :] + cd[2 * cs:, :]
            st_ref[d] = states[d] * decay + res[d][:2 * cs, :] + cd[:2 * cs, :]
        return carry

    lax.fori_loop(0, n_chunks // FACTOR_CHUNKS, factor, 0)
    lax.fori_loop(0, n_chunks, scan, 0)
    o = yf_s[...] + yb_s[...]
    mean = _seg_sum2(o) * (1.0 / n)
    cen = o - mean
    var = _seg_sum2(cen * cen) * (1.0 / n)
    y = cen * lax.rsqrt(var + RWKV_LN_EPS) * lnw_ref[...] + lnb_ref[...]
    bonus = _seg_sum2(r_ref[...] * k_ref[...] * rk_ref[...]) * v_ref[...]
    o_ref[...] = y + bonus


def _rwkv_core(r, k, v, kk, a, lw0, lw1, r_k, lnw, lnb):
    b, l, d = r.shape
    cs = RWKV_CHUNK
    pair = pl.BlockSpec((None, l, LANES), lambda i, j: (i, 0, j))
    vec = pl.BlockSpec((1, LANES), lambda i, j: (0, j))
    assert l % (cs * FACTOR_CHUNKS) == 0, (l, cs, FACTOR_CHUNKS)
    return pl.pallas_call(
        functools.partial(_rwkv_core_body, cs=cs, n_chunks=l // cs),
        grid=(b, d // LANES),
        in_specs=[pair] * 7 + [vec] * 3,
        out_specs=pair,
        out_shape=jax.ShapeDtypeStruct((b, l, d), F32),
        scratch_shapes=[pltpu.VMEM((l, LANES), F32), pltpu.VMEM((l, LANES), F32),
                        pltpu.VMEM((2, 3 * l, LANES), BF16),
                        pltpu.VMEM((2, 3 * l, LANES), F32),
                        pltpu.VMEM((2, 2 * l, LANES), F32),
                        pltpu.VMEM((l, LANES), F32), pltpu.VMEM((l, LANES), F32),
                        pltpu.VMEM((2, LANES, LANES), F32)],
        compiler_params=_cparams("parallel", "parallel"),
        name="rwkv_core",
    )(r, k, v, kk, a, lw0, lw1, r_k, lnw, lnb)


def _gate_proj_body(y_ref, gate_ref, w_ref, x_ref, o_ref):
    o_ref[...] = x_ref[...] + jnp.dot((y_ref[...] * gate_ref[...]).astype(BF16), w_ref[...],
                                      preferred_element_type=F32)


def _gate_proj(y, gate, w, x):
    t, d = x.shape
    tm = TOKEN_TILE
    tile = pl.BlockSpec((tm, d), lambda i: (i, 0))
    return pl.pallas_call(
        _gate_proj_body,
        grid=(t // tm,),
        in_specs=[tile, tile, _resident(w.shape), tile],
        out_specs=tile,
        out_shape=jax.ShapeDtypeStruct((t, d), F32),
        compiler_params=_cparams("parallel"),
        name="rwkv_out",
    )(y, gate, w, x)


def _rwkv_mixer(x, b, l, g_norm, mu, wr, wk, wv, w1, w2, w0, a0, a1, a2, g1, g2, k_k, k_a, r_k, lnw, lnb, w_out):
    t, d = x.shape
    r, k, v, kk, a, lw0, lw1, gate = _rwkv_prep(x.reshape(b, l, d), g_norm, mu, wr, wk, wv, w1, w2, w0,
                                                a0, a1, a2, g1, g2, k_k, k_a)
    y = _rwkv_core(r, k, v, kk, a, lw0, lw1, r_k, lnw, lnb)
    return _gate_proj(y.reshape(t, d), gate.reshape(t, d), w_out, x)


def _row(v):
    return v.reshape(1, -1).astype(F32)


def _pad_cols(w, n):
    return jnp.pad(w, ((0, 0), (0, n - w.shape[1])))


def _pad_row(v, n):
    v = v.reshape(1, -1).astype(F32)
    return jnp.pad(v, ((0, 0), (0, n - v.shape[1])))


def _ffn_weights(w_gu, w_down):
    d, f2 = w_gu.shape
    f = f2 // 2
    nck = f // FFN_CHUNK
    wg3 = jnp.transpose(w_gu[:, :f].reshape(d, nck, FFN_CHUNK), (1, 0, 2)).astype(BF16)
    wu3 = jnp.transpose(w_gu[:, f:].reshape(d, nck, FFN_CHUNK), (1, 0, 2)).astype(BF16)
    wd3 = w_down.reshape(nck, FFN_CHUNK, d).astype(BF16)
    return wg3, wu3, wd3


def _rope_tables(l):
    half = ATT_HEAD_DIM // 2
    inv_freq = ROPE_THETA ** (-jnp.arange(half, dtype=F32) / half)
    ang = jnp.arange(l).astype(F32)[:, None] * inv_freq[None, :]
    cos, sin = jnp.cos(ang), jnp.sin(ang)
    reps = LANES // ATT_HEAD_DIM
    cos2 = jnp.tile(jnp.concatenate([cos, cos], axis=1), (1, reps))
    sin2 = jnp.tile(jnp.concatenate([-sin, sin], axis=1), (1, reps))
    return cos2, sin2


def kernel(x_prompt, x_sample, ffn1_norm, ffn1_w_gu, ffn1_w_down, mix_norm, ffn2_norm, ffn2_w_gu, ffn2_w_down, ssd_w_in, ssd_conv_w, ssd_conv_b, ssd_a_log, ssd_dt_bias, ssd_d, ssd_norm, ssd_w_out, gdn_w_in, gdn_conv_w, gdn_conv_b, gdn_a_log, gdn_dt_bias, gdn_norm, gdn_w_out, att_w_qkv, att_sinks, att_w_out, rwkv_x_mu, rwkv_w_rkv, rwkv_w0, rwkv_w1, rwkv_w2, rwkv_a0, rwkv_a1, rwkv_a2, rwkv_g1, rwkv_g2, rwkv_k_k, rwkv_k_a, rwkv_r_k, rwkv_lnx_w, rwkv_lnx_b, rwkv_w_out, final_norm):
    depth = ffn1_norm.shape[0]
    ffn1 = [_ffn_weights(ffn1_w_gu[i], ffn1_w_down[i]) for i in range(depth)]
    ffn2 = [_ffn_weights(ffn2_w_gu[i], ffn2_w_down[i]) for i in range(depth)]
    fin_g = _row(final_norm)
    hv = GDN_N_V_HEADS
    att_steps = ATT_N_KV_HEADS * ATT_HEAD_DIM // LANES

    def mixer(h, b, l, i):
        m, j = i % N_MIXERS, i // N_MIXERS
        g = _row(mix_norm[i])
        if m == 0:
            return _ssd_mixer(h, b, l, g, _pad_cols(ssd_w_in[j], PROJ_PAD).astype(BF16), ssd_conv_w[j],
                              _row(ssd_conv_b[j]), _pad_row(ssd_a_log[j], LANES), _pad_row(ssd_dt_bias[j], LANES),
                              _row(jnp.repeat(ssd_d[j], SSD_HEAD_DIM)), _row(ssd_norm[j]), ssd_w_out[j].astype(BF16))
        if m == 1:
            lead = jnp.zeros((1, hv), F32)
            alog = jnp.pad(jnp.concatenate([lead, _row(gdn_a_log[j])], axis=1), ((0, 0), (0, LANES - 3 * hv)))
            bias = jnp.pad(jnp.concatenate([lead, _row(gdn_dt_bias[j])], axis=1), ((0, 0), (0, LANES - 3 * hv)))
            return _gdn_mixer(h, b, l, g, _pad_cols(gdn_w_in[j], PROJ_PAD).astype(BF16), gdn_conv_w[j],
                              _row(gdn_conv_b[j]), alog, bias, _row(gdn_norm[j]), gdn_w_out[j].astype(BF16))
        if m == 2:
            cos2, sin2 = _rope_tables(l)
            return _att_mixer(h, b, l, g, att_w_qkv[j].astype(BF16), cos2, sin2,
                              att_sinks[j].astype(F32).reshape(att_steps, 1, -1), att_w_out[j].astype(BF16))
        w1 = jnp.concatenate([rwkv_w1[j, 0], rwkv_w1[j, 1]], axis=1).astype(BF16)
        return _rwkv_mixer(h, b, l, g, rwkv_x_mu[j], rwkv_w_rkv[j, 0].astype(BF16), rwkv_w_rkv[j, 1].astype(BF16),
                           rwkv_w_rkv[j, 2].astype(BF16), w1, rwkv_w2[j].astype(BF16), rwkv_w0[j],
                           _row(rwkv_a0[j]), rwkv_a1[j].astype(BF16), rwkv_a2[j].astype(BF16),
                           rwkv_g1[j].astype(BF16), rwkv_g2[j].astype(BF16), _row(rwkv_k_k[j]), _row(rwkv_k_a[j]),
                           _row(rwkv_r_k[j]), _row(rwkv_lnx_w[j]), _row(rwkv_lnx_b[j]), rwkv_w_out[j].astype(BF16))

    def trunk(x3):
        b, l, d = x3.shape
        x = x3.reshape(b * l, d)
        for i in range(depth):
            x = _ffn(x, _row(ffn1_norm[i]), *ffn1[i], fin_g, False)
            x = mixer(x, b, l, i)
            x = _ffn(x, _row(ffn2_norm[i]), *ffn2[i], fin_g, i == depth - 1)
        return x.reshape(b, l, d)

    return (trunk(x_prompt), trunk(x_sample))
```

```python
import functools
import math

import jax
import jax.numpy as jnp
from jax import lax
from jax.experimental import pallas as pl
from jax.experimental.pallas import tpu as pltpu

F32 = jnp.float32
BF16 = jnp.bfloat16

V7X_VMEM_LIMIT_BYTES = 56 * 1024 * 1024
LANES = 128

RMS_EPS = 1e-6
CONV_K = 5
N_MIXERS = 4

SSD_HEAD_DIM = 64
SSD_N_GROUPS = 8
SSD_HEADS_PER_GROUP = 4
SSD_D_STATE = 128
SSD_CHUNK = 128
SSD_D_INNER = 2048

GDN_N_K_HEADS = 8
GDN_N_V_HEADS = 16
GDN_HEAD = 128
GDN_CHUNK = 64
GDN_QK_DIM = 1024
GDN_V_DIM = 2048

ATT_N_HEADS = 16
ATT_N_KV_HEADS = 4
ATT_HEAD_DIM = 64
ATT_WINDOW = 128
ROPE_THETA = 10000.0
ATT_BLOCKS_PER_STEP = 4

RWKV_HEAD = 64
RWKV_CHUNK = 64
RWKV_LN_EPS = 64e-5

PROJ_AUX = 256
PROJ_PAD = 6144 + PROJ_AUX
PROJ_CHUNK = 512

TOKEN_TILE = 512
FFN_TOKEN_TILE = 1024
FFN_CHUNK = 256
FACTOR_CHUNKS = 8


def _cparams(*sem):
    return pltpu.CompilerParams(dimension_semantics=sem, vmem_limit_bytes=V7X_VMEM_LIMIT_BYTES)


def _resident(shape):
    nd = len(shape)
    return pl.BlockSpec(shape, lambda *_: (0,) * nd, pipeline_mode=pl.Buffered(1))


def _sigmoid(x):
    return 1.0 / (1.0 + jnp.exp(-x))


def _silu(x):
    return x * _sigmoid(x)


def _softplus(x):
    return jnp.maximum(x, 0.0) + jnp.log1p(jnp.exp(-jnp.abs(x)))


def _rms(x, g):
    return x * lax.rsqrt(jnp.mean(x * x, axis=-1, keepdims=True) + RMS_EPS) * g


def _dot(a, b):
    return jnp.dot(a.astype(BF16), b.astype(BF16), preferred_element_type=F32)


def _dot_nt(a, b):
    return lax.dot_general(a.astype(BF16), b.astype(BF16), (((1,), (1,)), ((), ())), preferred_element_type=F32)


def _dot_f32(mask, x):
    m = mask.astype(BF16)
    hi = x.astype(BF16)
    rest = x - hi.astype(F32)
    mid = rest.astype(BF16)
    lo = (rest - mid.astype(F32)).astype(BF16)
    return (jnp.dot(m, hi, preferred_element_type=F32) + jnp.dot(m, mid, preferred_element_type=F32)
            + jnp.dot(m, lo, preferred_element_type=F32))


def _tri_apply_many(ns, xs, size):
    steps = int(math.log2(size))
    ms = list(ns)
    xs = list(xs)
    for s in range(steps):
        prods = [_dot(m, x) for m, x in zip(ms, xs)]
        if s + 1 < steps:
            ms = [_dot(m, m) for m in ms]
        xs = [x + p for x, p in zip(xs, prods)]
    return xs


def _dwconv_silu(x, w, b):
    n = x.shape[0]
    half = CONV_K // 2

    def taps(xx, masked):
        m = xx.shape[0]
        acc = xx * w[half:half + 1, :] + b
        for k in range(-half, half + 1):
            if k == 0:
                continue
            xr = pltpu.roll(xx, (-k) % m, axis=0)
            if masked:
                row = lax.broadcasted_iota(jnp.int32, xx.shape, 0)
                xr = jnp.where((row + k >= 0) & (row + k < m), xr, 0.0)
            acc = acc + xr * w[k + half:k + half + 1, :]
        return acc

    edge = 8
    head = taps(x[:2 * edge, :], True)[:edge, :]
    tail = taps(x[n - 2 * edge:, :], True)[edge:, :]
    acc = jnp.concatenate([head, taps(x, False)[edge:n - edge, :], tail], axis=0)
    return _silu(acc)


def _seg_sum2(x):
    lane = lax.broadcasted_iota(jnp.int32, x.shape, 1)
    lo = lane < 64
    s_lo = jnp.sum(jnp.where(lo, x, 0.0), axis=1, keepdims=True)
    s_hi = jnp.sum(jnp.where(lo, 0.0, x), axis=1, keepdims=True)
    return jnp.where(lo, s_lo, s_hi)


def _tri_masks(n, forward):
    li = lax.broadcasted_iota(jnp.int32, (n, n), 0)
    si = lax.broadcasted_iota(jnp.int32, (n, n), 1)
    if forward:
        return li >= si, li > si
    return li <= si, li < si


def _ffn_body(x_ref, g_ref, wg_ref, wu_ref, wd_ref, fg_ref, o_ref, xn_ref, acc_ref, *, n_chunks, final):
    x = x_ref[...]
    xn_ref[...] = _rms(x, g_ref[...]).astype(BF16)

    def down(c):
        xb = xn_ref[...]
        gate = jnp.dot(xb, wg_ref[c], preferred_element_type=F32)
        up = jnp.dot(xb, wu_ref[c], preferred_element_type=F32)
        h = (_silu(gate) * up).astype(BF16)
        return jnp.dot(h, wd_ref[c], preferred_element_type=F32)

    def chunk(c, carry):
        acc_ref[...] += down(c)
        return carry

    acc_ref[...] = down(0)
    lax.fori_loop(1, n_chunks, chunk, 0)
    y = x + 0.5 * acc_ref[...]
    if final:
        y = _rms(y, fg_ref[...])
    o_ref[...] = y


def _ffn(x, g, wg3, wu3, wd3, final_g, final):
    t, d = x.shape
    tm = FFN_TOKEN_TILE
    body = functools.partial(_ffn_body, n_chunks=wg3.shape[0], final=final)
    return pl.pallas_call(
        body,
        grid=(t // tm,),
        in_specs=[pl.BlockSpec((tm, d), lambda i: (i, 0)),
                  _resident((1, d)), _resident(wg3.shape), _resident(wu3.shape), _resident(wd3.shape),
                  _resident((1, d))],
        out_specs=pl.BlockSpec((tm, d), lambda i: (i, 0)),
        out_shape=jax.ShapeDtypeStruct((t, d), F32),
        scratch_shapes=[pltpu.VMEM((tm, d), BF16), pltpu.VMEM((tm, d), F32)],
        compiler_params=_cparams("parallel"),
        name="ffn",
    )(x, g, wg3, wu3, wd3, final_g)


def _proj_body(x_ref, g_ref, w_ref, *out_refs, n_main):
    xn = _rms(x_ref[...], g_ref[...]).astype(BF16)
    main_ref = out_refs[0]
    for j in range(0, n_main, PROJ_CHUNK):
        main_ref[:, j:j + PROJ_CHUNK] = jnp.dot(xn, w_ref[:, j:j + PROJ_CHUNK],
                                                preferred_element_type=F32).astype(BF16)
    if len(out_refs) > 1:
        out_refs[1][...] = jnp.dot(xn, w_ref[:, n_main:], preferred_element_type=F32)


def _proj(x, g, w, n_aux):
    t, d = x.shape
    n_main = w.shape[1] - n_aux
    tm = TOKEN_TILE
    out_specs = [pl.BlockSpec((tm, n_main), lambda i: (i, 0))]
    out_shape = [jax.ShapeDtypeStruct((t, n_main), BF16)]
    if n_aux:
        out_specs.append(pl.BlockSpec((tm, n_aux), lambda i: (i, 0)))
        out_shape.append(jax.ShapeDtypeStruct((t, n_aux), F32))
    return pl.pallas_call(
        functools.partial(_proj_body, n_main=n_main),
        grid=(t // tm,),
        in_specs=[pl.BlockSpec((tm, d), lambda i: (i, 0)), _resident((1, d)), _resident(w.shape)],
        out_specs=out_specs,
        out_shape=out_shape,
        compiler_params=_cparams("parallel"),
        name="norm_proj",
    )(x, g, w)


def _ssd_prep_body(p_ref, bias_ref, alog_ref, dt_ref, cs_ref, *, q):
    raw = p_ref[:, 0:LANES]
    dt = _softplus(raw + bias_ref[...])
    dta = dt * (-jnp.exp(alog_ref[...]))
    dt_ref[...] = dt
    incl_f, _ = _tri_masks(q, True)
    incl_b, _ = _tri_masks(q, False)
    tril = incl_f.astype(F32)
    triu = incl_b.astype(F32)
    lane = lax.broadcasted_iota(jnp.int32, (q, LANES), 1)
    n_heads = SSD_N_GROUPS * SSD_HEADS_PER_GROUP
    for c in range(raw.shape[0] // q):
        blk = dta[c * q:(c + 1) * q, :]
        cs_ref[c * q:(c + 1) * q, :] = jnp.where(lane < n_heads, _dot_f32(tril, blk), _dot_f32(triu, blk))


def _ssd_prep(p, bias, alog):
    t = p.shape[0]
    tm = TOKEN_TILE
    blk = pl.BlockSpec((tm, LANES), lambda i: (i, 0))
    return pl.pallas_call(
        functools.partial(_ssd_prep_body, q=SSD_CHUNK),
        grid=(t // tm,),
        in_specs=[pl.BlockSpec((tm, PROJ_AUX), lambda i: (i, 0)),
                  _resident((1, LANES)), _resident((1, LANES))],
        out_specs=[blk, blk],
        out_shape=[jax.ShapeDtypeStruct((t, LANES), F32)] * 2,
        compiler_params=_cparams("parallel"),
        name="ssd_prep",
    )(p, bias, alog)


def _ssd_core_body(xs_ref, bm_ref, cm_ref, wx_ref, wb_ref, wc_ref, bx_ref, bb_ref, bc_ref, dsk_ref,
                   dtr_ref, csr_ref, csc_ref, o_ref, xs_s, b_s, c_s, yf_s, yb_s, st_ref, *, q, n_chunks):
    hd, nh = SSD_HEAD_DIM, SSD_HEADS_PER_GROUP
    xs_s[...] = _dwconv_silu(xs_ref[...].astype(F32), wx_ref[...], bx_ref[...])
    b_s[...] = _dwconv_silu(bm_ref[...].astype(F32), wb_ref[...], bb_ref[...])
    c_s[...] = _dwconv_silu(cm_ref[...].astype(F32), wc_ref[...], bc_ref[...]).astype(BF16)
    st_ref[...] = jnp.zeros_like(st_ref)
    sh = int(math.log2(hd))
    lane_head = jnp.right_shift(lax.broadcasted_iota(jnp.int32, (q, nh * hd), 1), sh)
    lane_head_row = jnp.right_shift(lax.broadcasted_iota(jnp.int32, (1, nh * hd), 1), sh)

    def per_head(vals, lane_map):
        out = vals[nh - 1]
        for h in range(nh - 2, -1, -1):
            out = jnp.where(lane_map == h, vals[h], out)
        return out

    def step(i, carry):
        prep = []
        for d, forward in enumerate((True, False)):
            c = i if forward else n_chunks - 1 - i
            r0 = pl.multiple_of(c * q, q)
            incl, _ = _tri_masks(q, forward)
            cc = c_s[pl.ds(r0, q), :]
            bc = b_s[pl.ds(r0, q), :]
            cb = _dot_nt(cc, bc)
            bct = bc.T
            xc = xs_s[pl.ds(r0, q), :].astype(BF16)
            acols = csc_ref[pl.ds(r0, q), :]
            arows = csr_ref[c]
            dtrows = dtr_ref[c]
            lmats, wmats, eas, tots = [], [], [], []
            for h in range(nh):
                k = d * nh + h
                acol = jnp.broadcast_to(acols[:, k:k + 1], (q, q))
                arow = arows[k:k + 1, :]
                dtr = dtrows[k:k + 1, :]
                tot = arow[:, q - 1:q] if forward else arow[:, 0:1]
                lmats.append((jnp.where(incl, jnp.exp(acol - arow), 0.0) * cb * dtr).astype(BF16))
                wmats.append((bct * (jnp.exp(tot - arow) * dtr)).astype(BF16))
                eas.append(jnp.exp(jnp.concatenate([acol] * (nh * hd // q), axis=1)))
                tots.append(jnp.exp(tot))
            xms = [jnp.where(lane_head == h, xc, jnp.zeros_like(xc)) for h in range(nh)]
            prep.append((r0, cc, lmats, wmats, xms, per_head(eas, lane_head), per_head(tots, lane_head_row)))
        states = [st_ref[d] for d in range(2)]
        y_off = [jnp.dot(p[1], s.astype(BF16), preferred_element_type=F32) for p, s in zip(prep, states)]
        y_diag = [sum(jnp.dot(lm, xm, preferred_element_type=F32) for lm, xm in zip(p[2], p[4])) for p in prep]
        upd = [sum(jnp.dot(wm, xm, preferred_element_type=F32) for wm, xm in zip(p[3], p[4])) for p in prep]
        for d, y_s in enumerate((yf_s, yb_s)):
            y_s[pl.ds(prep[d][0], q), :] = y_diag[d] + y_off[d] * prep[d][5]
            st_ref[d] = states[d] * prep[d][6] + upd[d]
        return carry

    lax.fori_loop(0, n_chunks, step, 0)
    o_ref[...] = yf_s[...] + yb_s[...] + xs_s[...] * dsk_ref[...]


def _ssd_core(p3, conv_w, conv_b, dskip, dt_rows, cs_rows, cs_cols):
    b, l, _ = p3.shape
    q = SSD_CHUNK
    nc = l // q
    gw = SSD_HEADS_PER_GROUP * SSD_HEAD_DIM
    ns = SSD_D_STATE
    x0 = SSD_D_INNER // gw
    b0 = 2 * SSD_D_INNER // ns
    c0 = b0 + SSD_N_GROUPS
    wb0 = SSD_D_INNER // ns
    wc0 = wb0 + SSD_N_GROUPS
    rows_spec = pl.BlockSpec((None, None, nc, 8, q), lambda i, g: (i, g, 0, 0, 0))
    return pl.pallas_call(
        functools.partial(_ssd_core_body, q=q, n_chunks=nc),
        grid=(b, SSD_N_GROUPS),
        in_specs=[pl.BlockSpec((None, l, gw), lambda i, g: (i, 0, x0 + g)),
                  pl.BlockSpec((None, l, ns), lambda i, g: (i, 0, b0 + g)),
                  pl.BlockSpec((None, l, ns), lambda i, g: (i, 0, c0 + g)),
                  pl.BlockSpec((CONV_K, gw), lambda i, g: (0, g)),
                  pl.BlockSpec((CONV_K, ns), lambda i, g: (0, wb0 + g)),
                  pl.BlockSpec((CONV_K, ns), lambda i, g: (0, wc0 + g)),
                  pl.BlockSpec((1, gw), lambda i, g: (0, g)),
                  pl.BlockSpec((1, ns), lambda i, g: (0, wb0 + g)),
                  pl.BlockSpec((1, ns), lambda i, g: (0, wc0 + g)),
                  pl.BlockSpec((1, gw), lambda i, g: (0, g)),
                  rows_spec, rows_spec,
                  pl.BlockSpec((None, None, l, 8), lambda i, g: (i, g, 0, 0))],
        out_specs=pl.BlockSpec((None, l, gw), lambda i, g: (i, 0, g)),
        out_shape=jax.ShapeDtypeStruct((b, l, SSD_D_INNER), F32),
        scratch_shapes=[pltpu.VMEM((l, gw), F32), pltpu.VMEM((l, ns), F32), pltpu.VMEM((l, ns), BF16),
                        pltpu.VMEM((l, gw), F32), pltpu.VMEM((l, gw), F32), pltpu.VMEM((2, ns, gw), F32)],
        compiler_params=_cparams("parallel", "parallel"),
        name="ssd_core",
    )(p3, p3, p3, conv_w, conv_w, conv_w, conv_b, conv_b, conv_b, dskip, dt_rows, cs_rows, cs_cols)


def _ssd_out_body(y_ref, z_ref, nw_ref, w_ref, x_ref, o_ref):
    yz = y_ref[...] * _silu(z_ref[...].astype(F32))
    o_ref[...] = x_ref[...] + jnp.dot(_rms(yz, nw_ref[...]).astype(BF16), w_ref[...], preferred_element_type=F32)


def _ssd_out(y, p, norm_w, w_out, x):
    t, d = x.shape
    tm = TOKEN_TILE
    di = SSD_D_INNER
    return pl.pallas_call(
        _ssd_out_body,
        grid=(t // tm,),
        in_specs=[pl.BlockSpec((tm, di), lambda i: (i, 0)),
                  pl.BlockSpec((tm, di), lambda i: (i, 0)),
                  _resident((1, di)), _resident(w_out.shape),
                  pl.BlockSpec((tm, d), lambda i: (i, 0))],
        out_specs=pl.BlockSpec((tm, d), lambda i: (i, 0)),
        out_shape=jax.ShapeDtypeStruct((t, d), F32),
        compiler_params=_cparams("parallel"),
        name="ssd_out",
    )(y, p, norm_w, w_out, x)


def _group_rows_cols(a, b, l, nc, q):
    g = a.shape[2]
    cols = jnp.transpose(a, (0, 2, 1, 3))
    rows = jnp.transpose(a.reshape(b, nc, q, g, 8), (0, 3, 1, 4, 2))
    return rows, cols


def _ssd_mixer(x, b, l, g_norm, w_in, conv_w, conv_b, a_log, dt_bias, d_skip, norm_w, w_out):
    t = b * l
    q = SSD_CHUNK
    nc = l // q
    p, aux = _proj(x, g_norm, w_in, PROJ_AUX)
    dt, cs = _ssd_prep(aux, dt_bias, a_log)
    ng, nh = SSD_N_GROUPS, SSD_HEADS_PER_GROUP

    def regroup(a):
        a = a[:, :2 * ng * nh].reshape(b, l, 2, ng, nh)
        return jnp.transpose(a, (0, 1, 3, 2, 4)).reshape(b, l, ng, 2 * nh)

    dt_rows, _ = _group_rows_cols(regroup(dt), b, l, nc, q)
    cs_rows, cs_cols = _group_rows_cols(regroup(cs), b, l, nc, q)
    y = _ssd_core(p.reshape(b, l, -1), conv_w, conv_b, d_skip, dt_rows, cs_rows, cs_cols)
    return _ssd_out(y.reshape(t, SSD_D_INNER), p, norm_w, w_out, x)


def _gdn_prep_body(p_ref, bias_ref, alog_ref, o_ref, *, cs):
    raw = p_ref[:, 0:LANES]
    beta = _sigmoid(raw)
    g = -jnp.exp(alog_ref[...]) * _softplus(raw + bias_ref[...])
    n = LANES
    li = lax.broadcasted_iota(jnp.int32, (n, n), 0)
    si = lax.broadcasted_iota(jnp.int32, (n, n), 1)
    sh = int(math.log2(cs))
    same = jnp.right_shift(li, sh) == jnp.right_shift(si, sh)
    tril = (same & (li >= si)).astype(F32)
    triu = (same & (li <= si)).astype(F32)
    lane = lax.broadcasted_iota(jnp.int32, (n, LANES), 1)
    hv = GDN_N_V_HEADS
    for c in range(raw.shape[0] // n):
        blk = g[c * n:(c + 1) * n, :]
        o_ref[c * n:(c + 1) * n, :] = jnp.where(
            lane < hv, beta[c * n:(c + 1) * n, :],
            jnp.where(lane < 2 * hv, _dot_f32(tril, blk), _dot_f32(triu, blk)))


def _gdn_prep(p, bias, alog):
    t = p.shape[0]
    tm = TOKEN_TILE
    return pl.pallas_call(
        functools.partial(_gdn_prep_body, cs=GDN_CHUNK),
        grid=(t // tm,),
        in_specs=[pl.BlockSpec((tm, PROJ_AUX), lambda i: (i, 0)),
                  _resident((1, LANES)), _resident((1, LANES))],
        out_specs=pl.BlockSpec((tm, LANES), lambda i: (i, 0)),
        out_shape=jax.ShapeDtypeStruct((t, LANES), F32),
        compiler_params=_cparams("parallel"),
        name="gdn_prep",
    )(p, bias, alog)


def _gdn_core_body(q_ref, k_ref, v_ref, wq_ref, wk_ref, wv_ref, bq_ref, bk_ref, bv_ref, grow_ref, cols_ref,
                   o_ref, q_s, k_s, v_s, mg_s, c_s, d_s, yf_s, yb_s, st_ref, *, cs, n_chunks):
    hd = GDN_HEAD
    qn = _dwconv_silu(q_ref[...].astype(F32), wq_ref[...], bq_ref[...])
    qn = qn * lax.rsqrt(jnp.sum(qn * qn, axis=-1, keepdims=True) + 1e-6)
    q_s[...] = qn * (hd ** -0.5)
    kn = _dwconv_silu(k_ref[...].astype(F32), wk_ref[...], bk_ref[...])
    k_s[...] = kn * lax.rsqrt(jnp.sum(kn * kn, axis=-1, keepdims=True) + 1e-6)
    v_s[...] = _dwconv_silu(v_ref[...].astype(F32), wv_ref[...], bv_ref[...])
    st_ref[...] = jnp.zeros_like(st_ref)

    lane_blk = jnp.right_shift(lax.broadcasted_iota(jnp.int32, (cs, 4 * cs), 1), int(math.log2(cs)))
    t_row = lax.broadcasted_iota(jnp.int32, (cs, 4 * cs), 0)
    t_col = jnp.bitwise_and(lax.broadcasted_iota(jnp.int32, (cs, 4 * cs), 1), cs - 1)
    eye_packed = jnp.where(t_row == t_col, 1.0, 0.0)
    fwd_blk = lane_blk < 2
    incl_p = (fwd_blk & (t_row >= t_col)) | (jnp.logical_not(fwd_blk) & (t_row <= t_col))
    strict_p = (fwd_blk & (t_row > t_col)) | (jnp.logical_not(fwd_blk) & (t_row < t_col))
    head0_row = lax.broadcasted_iota(jnp.int32, (1, 2 * hd), 1) < hd
    half_row = lax.broadcasted_iota(jnp.int32, (1, 2 * cs), 1) < cs
    half3 = lax.broadcasted_iota(jnp.int32, (3 * cs, 2 * cs), 1) < cs

    def block_diag(z):
        return jnp.concatenate([jnp.where(lane_blk == r, z, 0.0) for r in range(4)], axis=0).astype(BF16)

    def pick(cols, idx):
        out = jnp.where(lane_blk == 2, cols[:, idx[2]:idx[2] + 1], cols[:, idx[3]:idx[3] + 1])
        out = jnp.where(lane_blk == 1, cols[:, idx[1]:idx[1] + 1], out)
        return jnp.where(lane_blk == 0, cols[:, idx[0]:idx[0] + 1], out)

    def chunk_end(g, forward):
        return (g[:, cs - 1:cs], g[:, 2 * cs - 1:2 * cs]) if forward else (g[:, 0:1], g[:, cs:cs + 1])

    def factor(i, carry):
        packs = []
        for j in range(FACTOR_CHUNKS):
            c = i * FACTOR_CHUNKS + j
            r0 = pl.multiple_of(c * cs, cs)
            c3 = pl.multiple_of(c * 3 * cs, cs)
            c4 = pl.multiple_of(c * 4 * cs, 4 * cs)
            qc = q_s[pl.ds(r0, cs), :]
            kc = k_s[pl.ds(r0, cs), :]
            vc = v_s[pl.ds(r0, cs), :]
            k4 = jnp.concatenate([kc] * 4, axis=0)
            packs.append((r0, c3, c4, qc, kc, vc, cols_ref[pl.ds(r0, cs), :], grow_ref[c],
                          _dot_nt(kc, k4), _dot_nt(qc, k4)))
        n_p, rhs_all, lhs_all, qd_all = [], [], [], []
        for r0, c3, c4, qc, kc, vc, cols, grow_p, kk4, qk4 in packs:
            dec_p = jnp.where(incl_p, jnp.exp(pick(cols, (2, 3, 4, 5)) - grow_p), 0.0)
            n_p.append(jnp.where(strict_p, -(pick(cols, (0, 1, 0, 1)) * kk4 * dec_p), 0.0))
            qkm = jnp.where(incl_p, qk4 * dec_p, 0.0)
            kt2 = jnp.concatenate([kc, kc], axis=0).T
            rhs, lhs, qds = [], [], []
            for d, forward in enumerate((True, False)):
                g = grow_p[:, d * 2 * cs:(d + 1) * 2 * cs]
                gl0, gl1 = chunk_end(g, forward)
                pair = jnp.concatenate([kt2 * jnp.exp(jnp.where(half_row, gl0, gl1) - g),
                                        qkm[:, d * 2 * cs:(d + 1) * 2 * cs]], axis=0)
                for e in range(2):
                    bcol = cols[:, e:e + 1]
                    eg = jnp.exp(cols[:, 2 + 2 * d + e:3 + 2 * d + e])
                    rhs.append(jnp.concatenate([vc[:, e * hd:(e + 1) * hd] * bcol, kc * (bcol * eg)],
                                               axis=1).astype(BF16))
                    lhs.append(jnp.where(half3 if e == 0 else jnp.logical_not(half3), pair, 0.0).astype(BF16))
                    qds.append(qc * eg)
            rhs_all.append(rhs)
            lhs_all.append(lhs)
            qd_all.append(qds)
        m_p = n_p
        inv_p = [eye_packed + m for m in n_p]
        steps = int(math.log2(cs))
        for s in range(1, steps):
            bds = [block_diag(m) for m in m_p]
            if s == 1:
                m_p = [jnp.dot(m.astype(BF16), bd, preferred_element_type=F32) for m, bd in zip(m_p, bds)]
                continue
            res = [jnp.dot(jnp.concatenate([p, m], axis=0).astype(BF16), bd, preferred_element_type=F32)
                   for p, m, bd in zip(inv_p, m_p, bds)]
            inv_p = [p + r[:cs, :] for p, r in zip(inv_p, res)]
            m_p = [r[cs:, :] for r in res]
        bds = [block_diag(m) for m in m_p]
        inv_p = [(p + jnp.dot(p.astype(BF16), bd, preferred_element_type=F32)).astype(BF16)
                 for p, bd in zip(inv_p, bds)]
        zero = jnp.zeros((cs, 2 * hd), BF16)
        sols = [[jnp.dot(p, jnp.concatenate([zero] * r + [rhs[r]] + [zero] * (3 - r), axis=0),
                         preferred_element_type=F32) for r in range(4)] for p, rhs in zip(inv_p, rhs_all)]
        folded = [[jnp.dot(lh[2 * d + e],
                           jnp.concatenate([sol[2 * d], sol[2 * d + 1]], axis=0).astype(BF16),
                           preferred_element_type=F32) for d in range(2) for e in range(2)]
                  for lh, sol in zip(lhs_all, sols)]
        for (r0, c3, c4, *_), fold, qds in zip(packs, folded, qd_all):
            c6 = pl.multiple_of(c3 * 2, 2 * cs)
            c2 = pl.multiple_of(r0 * 2, 2 * cs)
            for d in range(2):
                f0, f1 = fold[2 * d], fold[2 * d + 1]
                mg_s[d, pl.ds(c6, 6 * cs), :] = jnp.concatenate(
                    [-f0[:2 * cs, hd:], qds[2 * d] - f0[2 * cs:, hd:],
                     -f1[:2 * cs, hd:], qds[2 * d + 1] - f1[2 * cs:, hd:]], axis=0).astype(BF16)
                c_s[d, pl.ds(c2, 2 * cs), :] = jnp.concatenate([f0[:2 * cs, :hd], f1[:2 * cs, :hd]], axis=1)
                d_s[d, pl.ds(r0, cs), :] = jnp.concatenate([f0[2 * cs:, :hd], f1[2 * cs:, :hd]], axis=1)
        return carry

    def scan(i, carry):
        cidx = [i, n_chunks - 1 - i]
        states = [st_ref[d] for d in range(2)]
        res = [[jnp.dot(mg_s[d, pl.ds(pl.multiple_of(cidx[d] * 6 * cs, 2 * cs) + 3 * e * cs, 3 * cs), :],
                        states[d][:, e * hd:(e + 1) * hd].astype(BF16), preferred_element_type=F32)
                for e in range(2)] for d in range(2)]
        for d, (forward, y_s) in enumerate(((True, yf_s), (False, yb_s))):
            c = cidx[d]
            gl0, gl1 = chunk_end(grow_ref[c][:, d * 2 * cs:(d + 1) * 2 * cs], forward)
            r0, r1 = res[d]
            y_s[pl.ds(pl.multiple_of(c * cs, cs), cs), :] = (
                jnp.concatenate([r0[2 * cs:, :], r1[2 * cs:, :]], axis=1)
                + d_s[d, pl.ds(pl.multiple_of(c * cs, cs), cs), :])
            st_ref[d] = (states[d] * jnp.exp(jnp.where(head0_row, gl0, gl1))
                         + jnp.concatenate([r0[:2 * cs, :], r1[:2 * cs, :]], axis=1)
                         + c_s[d, pl.ds(pl.multiple_of(c * 2 * cs, 2 * cs), 2 * cs), :])
        return carry

    lax.fori_loop(0, n_chunks // FACTOR_CHUNKS, factor, 0)
    lax.fori_loop(0, n_chunks, scan, 0)
    o_ref[...] = yf_s[...] + yb_s[...]


def _gdn_core(p3, conv_w, conv_b, grow4, cols):
    b, l, _ = p3.shape
    cs = GDN_CHUNK
    nc = l // cs
    hd = GDN_HEAD
    vw = 2 * hd
    k0 = GDN_QK_DIM // hd
    v0 = 2 * GDN_QK_DIM // vw
    assert l % (cs * FACTOR_CHUNKS) == 0, (l, cs, FACTOR_CHUNKS)
    return pl.pallas_call(
        functools.partial(_gdn_core_body, cs=cs, n_chunks=nc),
        grid=(b, GDN_N_K_HEADS),
        in_specs=[pl.BlockSpec((None, l, hd), lambda i, j: (i, 0, j)),
                  pl.BlockSpec((None, l, hd), lambda i, j: (i, 0, k0 + j)),
                  pl.BlockSpec((None, l, vw), lambda i, j: (i, 0, v0 + j)),
                  pl.BlockSpec((CONV_K, hd), lambda i, j: (0, j)),
                  pl.BlockSpec((CONV_K, hd), lambda i, j: (0, k0 + j)),
                  pl.BlockSpec((CONV_K, vw), lambda i, j: (0, v0 + j)),
                  pl.BlockSpec((1, hd), lambda i, j: (0, j)),
                  pl.BlockSpec((1, hd), lambda i, j: (0, k0 + j)),
                  pl.BlockSpec((1, vw), lambda i, j: (0, v0 + j)),
                  pl.BlockSpec((None, None, nc, 1, 4 * cs), lambda i, j: (i, j, 0, 0, 0)),
                  pl.BlockSpec((None, None, l, 8), lambda i, j: (i, j, 0, 0))],
        out_specs=pl.BlockSpec((None, l, vw), lambda i, j: (i, 0, j)),
        out_shape=jax.ShapeDtypeStruct((b, l, GDN_V_DIM), F32),
        scratch_shapes=[pltpu.VMEM((l, hd), F32), pltpu.VMEM((l, hd), F32), pltpu.VMEM((l, vw), F32),
                        pltpu.VMEM((2, 6 * l, hd), BF16),
                        pltpu.VMEM((2, 2 * l, vw), F32),
                        pltpu.VMEM((2, l, vw), F32),
                        pltpu.VMEM((l, vw), F32), pltpu.VMEM((l, vw), F32), pltpu.VMEM((2, hd, vw), F32)],
        compiler_params=_cparams("parallel", "parallel"),
        name="gdn_core",
    )(p3, p3, p3, conv_w, conv_w, conv_w, conv_b, conv_b, conv_b, grow4, cols)


def _gdn_out_body(o_ref, z_ref, nw_ref, w_ref, x_ref, out_ref, h_s):
    hd = GDN_HEAD
    nw = nw_ref[...]
    for e in range(GDN_N_V_HEADS):
        oe = o_ref[:, e * hd:(e + 1) * hd]
        ze = z_ref[:, e * hd:(e + 1) * hd].astype(F32)
        h_s[:, e * hd:(e + 1) * hd] = (_rms(oe, nw) * _silu(ze)).astype(BF16)
    out_ref[...] = x_ref[...] + jnp.dot(h_s[...], w_ref[...], preferred_element_type=F32)


def _gdn_out(o, p, norm_w, w_out, x):
    t, d = x.shape
    tm = TOKEN_TILE
    dv = GDN_V_DIM
    return pl.pallas_call(
        _gdn_out_body,
        grid=(t // tm,),
        in_specs=[pl.BlockSpec((tm, dv), lambda i: (i, 0)),
                  pl.BlockSpec((tm, dv), lambda i: (i, 2 * GDN_QK_DIM // dv + 1)),
                  _resident((1, GDN_HEAD)), _resident(w_out.shape),
                  pl.BlockSpec((tm, d), lambda i: (i, 0))],
        out_specs=pl.BlockSpec((tm, d), lambda i: (i, 0)),
        out_shape=jax.ShapeDtypeStruct((t, d), F32),
        scratch_shapes=[pltpu.VMEM((tm, dv), BF16)],
        compiler_params=_cparams("parallel"),
        name="gdn_out",
    )(o, p, norm_w, w_out, x)


def _gdn_mixer(x, b, l, g_norm, w_in, conv_w, conv_b, a_log, dt_bias, norm_w, w_out):
    t = b * l
    cs = GDN_CHUNK
    nc = l // cs
    p, aux = _proj(x, g_norm, w_in, PROJ_AUX)
    gt = _gdn_prep(aux, dt_bias, a_log).reshape(b, l, LANES)
    hk, hv = GDN_N_K_HEADS, GDN_N_V_HEADS
    parts = [gt[..., i * hv:(i + 1) * hv].reshape(b, l, hk, 2) for i in range(3)]
    a = jnp.concatenate(parts + [jnp.zeros((b, l, hk, 2), F32)], axis=-1)
    cols = jnp.transpose(a, (0, 2, 1, 3))
    grow4 = jnp.transpose(a[..., 2:6].reshape(b, nc, cs, hk, 4), (0, 3, 1, 4, 2)).reshape(b, hk, nc, 1, 4 * cs)
    o = _gdn_core(p.reshape(b, l, -1), conv_w, conv_b, grow4, cols)
    return _gdn_out(o.reshape(t, GDN_V_DIM), p, norm_w, w_out, x)


def _att_core_body(q_ref, k_ref, v_ref, cos_ref, sin_ref, sink_ref, o_ref, q_s, k_s, v_s, band_s,
                   *, win, n_blocks, seq):
    dh = ATT_HEAD_DIM
    cos = cos_ref[...]
    sin = sin_ref[...]

    def rope(x):
        w = x.shape[1]
        lane = lax.broadcasted_iota(jnp.int32, x.shape, 1)
        first = jnp.bitwise_and(lane, dh - 1) < dh // 2
        partner = jnp.where(first, pltpu.roll(x, w - dh // 2, axis=1), pltpu.roll(x, dh // 2, axis=1))
        reps = w // LANES
        c = jnp.concatenate([cos] * reps, axis=1) if reps > 1 else cos
        s = jnp.concatenate([sin] * reps, axis=1) if reps > 1 else sin
        return x * c + partner * s

    n_heads = q_ref.shape[1] // dh
    rep = ATT_N_HEADS // ATT_N_KV_HEADS
    groups = range(n_heads // rep)
    qr = rope(q_ref[...].astype(F32)) * (dh ** -0.5)
    for h in range(n_heads):
        q_s[h] = qr[:, h * dh:(h + 1) * dh].astype(BF16)
    kr = rope(k_ref[...].astype(F32))
    vv = v_ref[...]
    ones = jnp.ones((seq, dh), BF16)
    for g in groups:
        for ref in (k_s, v_s):
            ref[g, pl.ds(0, win), :] = jnp.zeros((win, ref.shape[2]), BF16)
            ref[g, pl.ds(win + seq, win), :] = jnp.zeros((win, ref.shape[2]), BF16)
        k_s[g, pl.ds(win, seq), :] = kr[:, g * dh:(g + 1) * dh].astype(BF16)
        v_s[g, pl.ds(win, seq), :] = jnp.concatenate([vv[:, g * dh:(g + 1) * dh], ones], axis=1)
    qi = lax.broadcasted_iota(jnp.int32, (win, 3 * win), 0)
    ki = lax.broadcasted_iota(jnp.int32, (win, 3 * win), 1)
    band_s[...] = jnp.where(jnp.abs(ki - win - qi) <= win, 0.0, -jnp.inf)
    sinks = sink_ref[...]

    def block_pair(n, carry):
        starts = [pl.multiple_of((ATT_BLOCKS_PER_STEP * n + j) * win, win) for j in range(ATT_BLOCKS_PER_STEP)]
        biases = []
        for r0 in starts:
            kpos = r0 - win + lax.broadcasted_iota(jnp.int32, (1, 3 * win), 1)
            biases.append(band_s[...] + jnp.where((kpos >= 0) & (kpos < seq), 0.0, -jnp.inf))
        scores = [[_dot_nt(jnp.concatenate([q_s[g * rep + r, pl.ds(r0, win), :] for r in range(rep)], axis=0),
                           k_s[g, pl.ds(r0, 3 * win), :]) for g in groups] for r0 in starts]
        exps, maxes = [], []
        for j in range(ATT_BLOCKS_PER_STEP):
            exps.append([])
            maxes.append([])
            for g in groups:
                es = []
                for r in range(rep):
                    s = scores[j][g][r * win:(r + 1) * win, :] + biases[j]
                    sink = sinks[:, g * rep + r:g * rep + r + 1]
                    m = jnp.maximum(jnp.max(s, axis=-1, keepdims=True), sink)
                    es.append(jnp.exp(s - m).astype(BF16))
                    maxes[j].append(m)
                exps[j].append(jnp.concatenate(es, axis=0))
        nds = [[jnp.dot(exps[j][g], v_s[g, pl.ds(r0, 3 * win), :], preferred_element_type=F32) for g in groups]
               for j, r0 in enumerate(starts)]
        for j, r0 in enumerate(starts):
            outs = []
            for g in groups:
                for r in range(rep):
                    h = g * rep + r
                    nd = nds[j][g][r * win:(r + 1) * win, :]
                    outs.append(nd[:, :dh] / (nd[:, dh:] + jnp.exp(sinks[:, h:h + 1] - maxes[j][h])))
            o_ref[pl.ds(r0, win), :] = jnp.concatenate(outs, axis=1).astype(o_ref.dtype)
        return carry

    lax.fori_loop(0, n_blocks // ATT_BLOCKS_PER_STEP, block_pair, 0)


def _att_core(p3, cos2, sin2, sinks3):
    b, l, _ = p3.shape
    win = ATT_WINDOW
    dh = ATT_HEAD_DIM
    steps = ATT_N_KV_HEADS * dh // LANES
    qw = ATT_N_HEADS * dh // steps
    k0 = ATT_N_HEADS * dh // LANES
    v0 = k0 + steps
    assert l % (win * ATT_BLOCKS_PER_STEP) == 0, (l, win, ATT_BLOCKS_PER_STEP)
    return pl.pallas_call(
        functools.partial(_att_core_body, win=win, n_blocks=l // win, seq=l),
        grid=(b, steps),
        in_specs=[pl.BlockSpec((None, l, qw), lambda i, j: (i, 0, j)),
                  pl.BlockSpec((None, l, LANES), lambda i, j: (i, 0, k0 + j)),
                  pl.BlockSpec((None, l, LANES), lambda i, j: (i, 0, v0 + j)),
                  _resident((l, LANES)), _resident((l, LANES)),
                  pl.BlockSpec((None, 1, qw // dh), lambda i, j: (j, 0, 0))],
        out_specs=pl.BlockSpec((None, l, qw), lambda i, j: (i, 0, j)),
        out_shape=jax.ShapeDtypeStruct((b, l, ATT_N_HEADS * dh), BF16),
        scratch_shapes=[pltpu.VMEM((qw // dh, l, dh), BF16),
                        pltpu.VMEM((LANES // dh, l + 2 * win, dh), BF16),
                        pltpu.VMEM((LANES // dh, l + 2 * win, 2 * dh), BF16),
                        pltpu.VMEM((win, 3 * win), F32)],
        compiler_params=_cparams("parallel", "parallel"),
        name="att_core",
    )(p3, p3, p3, cos2, sin2, sinks3)


def _res_proj_body(a_ref, w_ref, x_ref, o_ref):
    o_ref[...] = x_ref[...] + jnp.dot(a_ref[...], w_ref[...], preferred_element_type=F32)


def _res_proj(a, w, x):
    t, d = x.shape
    k = a.shape[1]
    tm = TOKEN_TILE
    return pl.pallas_call(
        _res_proj_body,
        grid=(t // tm,),
        in_specs=[pl.BlockSpec((tm, k), lambda i: (i, 0)), _resident(w.shape),
                  pl.BlockSpec((tm, d), lambda i: (i, 0))],
        out_specs=pl.BlockSpec((tm, d), lambda i: (i, 0)),
        out_shape=jax.ShapeDtypeStruct((t, d), F32),
        compiler_params=_cparams("parallel"),
        name="res_proj",
    )(a, w, x)


def _att_mixer(x, b, l, g_norm, w_qkv, cos2, sin2, sinks3, w_out):
    t = b * l
    (p,) = _proj(x, g_norm, w_qkv, 0)
    o = _att_core(p.reshape(b, l, -1), cos2, sin2, sinks3)
    return _res_proj(o.reshape(t, -1), w_out, x)


def _rwkv_prep_body(x_ref, xp_ref, xn_ref, g_ref, mu_ref, wr_ref, wk_ref, wv_ref, w1_ref, w2_ref, w0_ref,
                    a0_ref, a1_ref, a2_ref, g1_ref, g2_ref, kk_ref, ka_ref,
                    r_o, k_o, v_o, kk_o, a_o, lw0_o, lw1_o, gate_o, *, n_tiles):
    i = pl.program_id(1)
    g = g_ref[...]
    u = _rms(x_ref[...], g)
    tl = u.shape[0]
    prev_row = jnp.where(i > 0, _rms(xp_ref[...], g)[7:8, :], 0.0)
    next_row = jnp.where(i < n_tiles - 1, _rms(xn_ref[...], g)[0:1, :], 0.0)
    row = lax.broadcasted_iota(jnp.int32, u.shape, 0)
    u_prev = jnp.where(row == 0, prev_row, pltpu.roll(u, 1, axis=0))
    u_next = jnp.where(row == tl - 1, next_row, pltpu.roll(u, tl - 1, axis=0))
    xx = 0.5 * (u_prev + u_next) - u
    mu = mu_ref[...]

    def mix(s):
        return (u + xx * mu[s:s + 1, :]).astype(BF16)

    r_o[...] = jnp.dot(mix(0), wr_ref[...], preferred_element_type=F32)
    k = jnp.dot(mix(1), wk_ref[...], preferred_element_type=F32)
    v_o[...] = jnp.dot(mix(2), wv_ref[...], preferred_element_type=F32)
    wl = jnp.tanh(jnp.dot(mix(3), w1_ref[...], preferred_element_type=F32))
    lora = w2_ref.shape[1]
    w0 = w0_ref[...]
    for d, out in enumerate((lw0_o, lw1_o)):
        z = w0[d:d + 1, :] + _dot(wl[:, d * lora:(d + 1) * lora], w2_ref[d])
        out[...] = -jnp.exp(-_softplus(-z) - 0.5)
    a = _sigmoid(a0_ref[...] + _dot(jnp.dot(mix(4), a1_ref[...], preferred_element_type=F32), a2_ref[...]))
    gate_o[...] = _dot(_sigmoid(jnp.dot(mix(5), g1_ref[...], preferred_element_type=F32)), g2_ref[...])
    a_o[...] = a
    kk_o[...] = k * kk_ref[...]
    k_o[...] = k * (1.0 + (a - 1.0) * ka_ref[...])


def _rwkv_prep(x3, g, mu, wr, wk, wv, w1, w2, w0, a0, a1, a2, g1, g2, k_k, k_a):
    b, l, d = x3.shape
    tl = 256
    nt = l // tl
    x4 = x3.reshape(b, l // 8, 8, d)
    tile = pl.BlockSpec((None, tl, d), lambda i, j: (i, j, 0))
    consts = [g, mu, wr, wk, wv, w1, w2, w0, a0, a1, a2, g1, g2, k_k, k_a]
    return pl.pallas_call(
        functools.partial(_rwkv_prep_body, n_tiles=nt),
        grid=(b, nt),
        in_specs=[tile,
                  pl.BlockSpec((None, None, 8, d), lambda i, j: (i, jnp.maximum(j * (tl // 8) - 1, 0), 0, 0)),
                  pl.BlockSpec((None, None, 8, d),
                               lambda i, j: (i, jnp.minimum((j + 1) * (tl // 8), l // 8 - 1), 0, 0))]
                 + [_resident(c.shape) for c in consts],
        out_specs=[tile] * 8,
        out_shape=[jax.ShapeDtypeStruct((b, l, d), F32)] * 8,
        compiler_params=_cparams("parallel", "parallel"),
        name="rwkv_prep",
    )(x3, x4, x4, *consts)


def _rwkv_core_body(r_ref, k_ref, v_ref, kk_ref, a_ref, lw0_ref, lw1_ref, rk_ref, lnw_ref, lnb_ref, o_ref,
                    kk_s, kb_s, mg_s, cd_s, pe_s, yf_s, yb_s, st_ref, *, cs, n_chunks):
    n = RWKV_HEAD
    kk = kk_ref[...]
    kk = kk * lax.rsqrt(_seg_sum2(kk * kk) + 1e-6)
    kk_s[...] = kk
    kb_s[...] = kk * a_ref[...]
    st_ref[...] = jnp.zeros_like(st_ref)

    lane_blk = jnp.right_shift(lax.broadcasted_iota(jnp.int32, (cs, 4 * cs), 1), int(math.log2(cs)))
    t_row = lax.broadcasted_iota(jnp.int32, (cs, 4 * cs), 0)
    t_col = jnp.bitwise_and(lax.broadcasted_iota(jnp.int32, (cs, 4 * cs), 1), cs - 1)
    eye_packed = jnp.where(t_row == t_col, 1.0, 0.0)
    head0 = lax.broadcasted_iota(jnp.int32, (cs, LANES), 1) < n
    head0_rows = lax.broadcasted_iota(jnp.int32, (LANES, cs), 0) < n

    def block_diag(z):
        return jnp.concatenate([jnp.where(lane_blk == r, z, 0.0) for r in range(4)], axis=0).astype(BF16)

    def split_heads_rows(z):
        return jnp.concatenate([jnp.where(head0, z, 0.0), jnp.where(head0, 0.0, z)], axis=0)

    def factor(i, carry):
        span = FACTOR_CHUNKS * cs
        base = pl.multiple_of(i * span, span)
        rows = pl.ds(base, span)
        rc2, kc2, vc2, kkc2, kbc2 = r_ref[rows, :], k_ref[rows, :], v_ref[rows, :], kk_s[rows, :], kb_s[rows, :]
        li = lax.broadcasted_iota(jnp.int32, (2 * cs, 2 * cs), 0)
        si = lax.broadcasted_iota(jnp.int32, (2 * cs, 2 * cs), 1)
        sh = int(math.log2(cs))
        same = jnp.right_shift(li, sh) == jnp.right_shift(si, sh)
        lws, cums = [], []
        for forward, lw_ref in ((True, lw0_ref), (False, lw1_ref)):
            lws.append(lw_ref[rows, :])
            tri = (same & ((li >= si) if forward else (li <= si))).astype(F32)
            cums.append(jnp.concatenate([_dot_f32(tri, lws[-1][p * 2 * cs:(p + 1) * 2 * cs, :])
                                         for p in range(FACTOR_CHUNKS // 2)], axis=0))
        units, lhss, rhss = [], [], []
        for j in range(FACTOR_CHUNKS):
            rs = slice(j * cs, (j + 1) * cs)
            c = i * FACTOR_CHUNKS + j
            r0 = pl.multiple_of(c * cs, cs)
            c2 = pl.multiple_of(c * 2 * cs, 2 * cs)
            c3 = pl.multiple_of(c * 3 * cs, cs)
            for d, forward in enumerate((True, False)):
                lw, cum = lws[d][rs, :], cums[d][rs, :]
                tot = cum[cs - 1:cs, :] if forward else cum[0:1, :]
                e_neg = jnp.exp(-cum)
                e_end = jnp.exp(tot - cum)
                a_hat = -kkc2[rs, :] * jnp.exp(cum - lw)
                r_hat = rc2[rs, :] * jnp.exp(cum)
                b_end_t = (kbc2[rs, :] * e_end).T
                k_end_t = (kc2[rs, :] * e_end).T
                upd_lhs = jnp.concatenate(
                    [jnp.where(head0_rows, b_end_t, 0.0), jnp.where(head0_rows, 0.0, b_end_t),
                     jnp.where(head0_rows, k_end_t, 0.0), jnp.where(head0_rows, 0.0, k_end_t)], axis=1)
                p_col = jnp.broadcast_to(jnp.exp(tot), (8, LANES)).T[:, 0:1]
                pe_s[d, pl.ds(c2, 2 * cs), :] = jnp.broadcast_to(p_col, (LANES, LANES))
                lhss.append(jnp.concatenate([a_hat, r_hat], axis=0))
                rhss.append(jnp.concatenate([split_heads_rows(kbc2[rs, :] * e_neg),
                                             split_heads_rows(kc2[rs, :] * e_neg)], axis=0))
                units.append((j, d, forward, r0, c2, c3, a_hat, r_hat, upd_lhs))
        grams = [_dot_nt(lh, rh) for lh, rh in zip(lhss, rhss)]
        tops = []
        lhs_all = []
        for (j, d, forward, r0, c2, c3, _, _, upd_lhs), gram in zip(units, grams):
            strict = (t_row > t_col) if forward else (t_row < t_col)
            incl = (t_row >= t_col) if forward else (t_row <= t_col)
            tops.append(jnp.where(strict, gram[:cs, :], 0.0))
            lhs_all.append(jnp.concatenate([jnp.where(incl, gram[cs:, :], 0.0), upd_lhs], axis=0).astype(BF16))
        n_p = [jnp.concatenate([tops[2 * j][:, :LANES], tops[2 * j + 1][:, :LANES]], axis=1)
               for j in range(FACTOR_CHUNKS)]
        ak_p = [jnp.concatenate([tops[2 * j][:, LANES:], tops[2 * j + 1][:, LANES:]], axis=1)
                for j in range(FACTOR_CHUNKS)]
        a_p = [jnp.concatenate([units[2 * j][6], units[2 * j + 1][6]], axis=1) for j in range(FACTOR_CHUNKS)]
        v_bd = [block_diag(jnp.concatenate([vc2[j * cs:(j + 1) * cs, :]] * 2, axis=1))
                for j in range(FACTOR_CHUNKS)]
        av_p = [jnp.dot(ak.astype(BF16), vb, preferred_element_type=F32) for ak, vb in zip(ak_p, v_bd)]
        m_p = n_p
        inv_p = [eye_packed + m for m in n_p]
        steps = int(math.log2(cs))
        for s in range(1, steps):
            bds = [block_diag(m) for m in m_p]
            if s == 1:
                m_p = [jnp.dot(m.astype(BF16), bd, preferred_element_type=F32) for m, bd in zip(m_p, bds)]
                continue
            res = [jnp.dot(jnp.concatenate([p, m], axis=0).astype(BF16), bd, preferred_element_type=F32)
                   for p, m, bd in zip(inv_p, m_p, bds)]
            inv_p = [p + r[:cs, :] for p, r in zip(inv_p, res)]
            m_p = [r[cs:, :] for r in res]
        bds = [block_diag(m) for m in m_p]
        inv_p = [p + jnp.dot(p.astype(BF16), bd, preferred_element_type=F32) for p, bd in zip(inv_p, bds)]
        sols = [jnp.dot(p.astype(BF16), jnp.concatenate([block_diag(a), block_diag(av)], axis=1),
                        preferred_element_type=F32) for p, a, av in zip(inv_p, a_p, av_p)]
        w_rows, uv_rows = [], []
        for u, (j, d, *_rest) in enumerate(units):
            w_rows.append(split_heads_rows(sols[j][:, d * LANES:(d + 1) * LANES]).astype(BF16))
            uv_rows.append(jnp.concatenate([split_heads_rows(sols[j][:, (2 + d) * LANES:(3 + d) * LANES]),
                                            split_heads_rows(vc2[j * cs:(j + 1) * cs, :])], axis=0).astype(BF16))
        by_w = [jnp.dot(lh[:, :2 * cs], w, preferred_element_type=F32) for lh, w in zip(lhs_all, w_rows)]
        by_u = [jnp.dot(lh, uv, preferred_element_type=F32) for lh, uv in zip(lhs_all, uv_rows)]
        for (j, d, _, r0, c2, c3, _, r_hat, _), bw, bu in zip(units, by_w, by_u):
            mg_s[d, pl.ds(c3, 3 * cs), :] = jnp.concatenate([bw[cs:, :], r_hat + bw[:cs, :]], axis=0).astype(BF16)
            cd_s[d, pl.ds(c3, 3 * cs), :] = jnp.concatenate([bu[cs:, :], bu[:cs, :]], axis=0)
        return carry

    def scan(i, carry):
        cidx = [i, n_chunks - 1 - i]
        states = [st_ref[d] for d in range(2)]
        res = [jnp.dot(mg_s[d, pl.ds(pl.multiple_of(cidx[d] * 3 * cs, cs), 3 * cs), :], states[d].astype(BF16),
                       preferred_element_type=F32) for d in range(2)]
        for d, y_s in enumerate((yf_s, yb_s)):
            cd = cd_s[d, pl.ds(pl.multiple_of(cidx[d] * 3 * cs, cs), 3 * cs), :]
            decay = pe_s[d, pl.ds(pl.multiple_of(cidx[d] * 2 * cs, 2 * cs), 2 * cs), :]
            y_s[pl.ds(pl.multiple_of(cidx[d] * cs, cs), cs), :] = res[d][2 * cs:, :] + cd[2 * cs:, :]
            st_ref[d] = states[d] * decay + res[d][:2 * cs, :] + cd[:2 * cs, :]
        return carry

    lax.fori_loop(0, n_chunks // FACTOR_CHUNKS, factor, 0)
    lax.fori_loop(0, n_chunks, scan, 0)
    o = yf_s[...] + yb_s[...]
    mean = _seg_sum2(o) * (1.0 / n)
    cen = o - mean
    var = _seg_sum2(cen * cen) * (1.0 / n)
    y = cen * lax.rsqrt(var + RWKV_LN_EPS) * lnw_ref[...] + lnb_ref[...]
    bonus = _seg_sum2(r_ref[...] * k_ref[...] * rk_ref[...]) * v_ref[...]
    o_ref[...] = y + bonus


def _rwkv_core(r, k, v, kk, a, lw0, lw1, r_k, lnw, lnb):
    b, l, d = r.shape
    cs = RWKV_CHUNK
    pair = pl.BlockSpec((None, l, LANES), lambda i, j: (i, 0, j))
    vec = pl.BlockSpec((1, LANES), lambda i, j: (0, j))
    assert l % (cs * FACTOR_CHUNKS) == 0, (l, cs, FACTOR_CHUNKS)
    return pl.pallas_call(
        functools.partial(_rwkv_core_body, cs=cs, n_chunks=l // cs),
        grid=(b, d // LANES),
        in_specs=[pair] * 7 + [vec] * 3,
        out_specs=pair,
        out_shape=jax.ShapeDtypeStruct((b, l, d), F32),
        scratch_shapes=[pltpu.VMEM((l, LANES), F32), pltpu.VMEM((l, LANES), F32),
                        pltpu.VMEM((2, 3 * l, LANES), BF16),
                        pltpu.VMEM((2, 3 * l, LANES), F32),
                        pltpu.VMEM((2, 2 * l, LANES), F32),
                        pltpu.VMEM((l, LANES), F32), pltpu.VMEM((l, LANES), F32),
                        pltpu.VMEM((2, LANES, LANES), F32)],
        compiler_params=_cparams("parallel", "parallel"),
        name="rwkv_core",
    )(r, k, v, kk, a, lw0, lw1, r_k, lnw, lnb)


def _gate_proj_body(y_ref, gate_ref, w_ref, x_ref, o_ref):
    o_ref[...] = x_ref[...] + jnp.dot((y_ref[...] * gate_ref[...]).astype(BF16), w_ref[...],
                                      preferred_element_type=F32)


def _gate_proj(y, gate, w, x):
    t, d = x.shape
    tm = TOKEN_TILE
    tile = pl.BlockSpec((tm, d), lambda i: (i, 0))
    return pl.pallas_call(
        _gate_proj_body,
        grid=(t // tm,),
        in_specs=[tile, tile, _resident(w.shape), tile],
        out_specs=tile,
        out_shape=jax.ShapeDtypeStruct((t, d), F32),
        compiler_params=_cparams("parallel"),
        name="rwkv_out",
    )(y, gate, w, x)


def _rwkv_mixer(x, b, l, g_norm, mu, wr, wk, wv, w1, w2, w0, a0, a1, a2, g1, g2, k_k, k_a, r_k, lnw, lnb, w_out):
    t, d = x.shape
    r, k, v, kk, a, lw0, lw1, gate = _rwkv_prep(x.reshape(b, l, d), g_norm, mu, wr, wk, wv, w1, w2, w0,
                                                a0, a1, a2, g1, g2, k_k, k_a)
    y = _rwkv_core(r, k, v, kk, a, lw0, lw1, r_k, lnw, lnb)
    return _gate_proj(y.reshape(t, d), gate.reshape(t, d), w_out, x)


def _row(v):
    return v.reshape(1, -1).astype(F32)


def _pad_cols(w, n):
    return jnp.pad(w, ((0, 0), (0, n - w.shape[1])))


def _pad_row(v, n):
    v = v.reshape(1, -1).astype(F32)
    return jnp.pad(v, ((0, 0), (0, n - v.shape[1])))


def _ffn_weights(w_gu, w_down):
    d, f2 = w_gu.shape
    f = f2 // 2
    nck = f // FFN_CHUNK
    wg3 = jnp.transpose(w_gu[:, :f].reshape(d, nck, FFN_CHUNK), (1, 0, 2)).astype(BF16)
    wu3 = jnp.transpose(w_gu[:, f:].reshape(d, nck, FFN_CHUNK), (1, 0, 2)).astype(BF16)
    wd3 = w_down.reshape(nck, FFN_CHUNK, d).astype(BF16)
    return wg3, wu3, wd3


def _rope_tables(l):
    half = ATT_HEAD_DIM // 2
    inv_freq = ROPE_THETA ** (-jnp.arange(half, dtype=F32) / half)
    ang = jnp.arange(l).astype(F32)[:, None] * inv_freq[None, :]
    cos, sin = jnp.cos(ang), jnp.sin(ang)
    reps = LANES // ATT_HEAD_DIM
    cos2 = jnp.tile(jnp.concatenate([cos, cos], axis=1), (1, reps))
    sin2 = jnp.tile(jnp.concatenate([-sin, sin], axis=1), (1, reps))
    return cos2, sin2


def kernel(x_prompt, x_sample, ffn1_norm, ffn1_w_gu, ffn1_w_down, mix_norm, ffn2_norm, ffn2_w_gu, ffn2_w_down, ssd_w_in, ssd_conv_w, ssd_conv_b, ssd_a_log, ssd_dt_bias, ssd_d, ssd_norm, ssd_w_out, gdn_w_in, gdn_conv_w, gdn_conv_b, gdn_a_log, gdn_dt_bias, gdn_norm, gdn_w_out, att_w_qkv, att_sinks, att_w_out, rwkv_x_mu, rwkv_w_rkv, rwkv_w0, rwkv_w1, rwkv_w2, rwkv_a0, rwkv_a1, rwkv_a2, rwkv_g1, rwkv_g2, rwkv_k_k, rwkv_k_a, rwkv_r_k, rwkv_lnx_w, rwkv_lnx_b, rwkv_w_out, final_norm):
    depth = ffn1_norm.shape[0]
    ffn1 = [_ffn_weights(ffn1_w_gu[i], ffn1_w_down[i]) for i in range(depth)]
    ffn2 = [_ffn_weights(ffn2_w_gu[i], ffn2_w_down[i]) for i in range(depth)]
    fin_g = _row(final_norm)
    hv = GDN_N_V_HEADS
    att_steps = ATT_N_KV_HEADS * ATT_HEAD_DIM // LANES

    def mixer(h, b, l, i):
        m, j = i % N_MIXERS, i // N_MIXERS
        g = _row(mix_norm[i])
        if m == 0:
            return _ssd_mixer(h, b, l, g, _pad_cols(ssd_w_in[j], PROJ_PAD).astype(BF16), ssd_conv_w[j],
                              _row(ssd_conv_b[j]), _pad_row(ssd_a_log[j], LANES), _pad_row(ssd_dt_bias[j], LANES),
                              _row(jnp.repeat(ssd_d[j], SSD_HEAD_DIM)), _row(ssd_norm[j]), ssd_w_out[j].astype(BF16))
        if m == 1:
            lead = jnp.zeros((1, hv), F32)
            alog = jnp.pad(jnp.concatenate([lead, _row(gdn_a_log[j])], axis=1), ((0, 0), (0, LANES - 3 * hv)))
            bias = jnp.pad(jnp.concatenate([lead, _row(gdn_dt_bias[j])], axis=1), ((0, 0), (0, LANES - 3 * hv)))
            return _gdn_mixer(h, b, l, g, _pad_cols(gdn_w_in[j], PROJ_PAD).astype(BF16), gdn_conv_w[j],
                              _row(gdn_conv_b[j]), alog, bias, _row(gdn_norm[j]), gdn_w_out[j].astype(BF16))
        if m == 2:
            cos2, sin2 = _rope_tables(l)
            return _att_mixer(h, b, l, g, att_w_qkv[j].astype(BF16), cos2, sin2,
                              att_sinks[j].astype(F32).reshape(att_steps, 1, -1), att_w_out[j].astype(BF16))
        w1 = jnp.concatenate([rwkv_w1[j, 0], rwkv_w1[j, 1]], axis=1).astype(BF16)
        return _rwkv_mixer(h, b, l, g, rwkv_x_mu[j], rwkv_w_rkv[j, 0].astype(BF16), rwkv_w_rkv[j, 1].astype(BF16),
                           rwkv_w_rkv[j, 2].astype(BF16), w1, rwkv_w2[j].astype(BF16), rwkv_w0[j],
                           _row(rwkv_a0[j]), rwkv_a1[j].astype(BF16), rwkv_a2[j].astype(BF16),
                           rwkv_g1[j].astype(BF16), rwkv_g2[j].astype(BF16), _row(rwkv_k_k[j]), _row(rwkv_k_a[j]),
                           _row(rwkv_r_k[j]), _row(rwkv_lnx_w[j]), _row(rwkv_lnx_b[j]), rwkv_w_out[j].astype(BF16))

    def trunk(x3):
        b, l, d = x3.shape
        x = x3.reshape(b * l, d)
        for i in range(depth):
            x = _ffn(x, _row(ffn1_norm[i]), *ffn1[i], fin_g, False)
            x = mixer(x, b, l, i)
            x = _ffn(x, _row(ffn2_norm[i]), *ffn2[i], fin_g, i == depth - 1)
        return x.reshape(b, l, d)

    return (trunk(x_prompt), trunk(x_sample))
```

```python
import functools
import math

import jax
import jax.numpy as jnp
from jax import lax
from jax.experimental import pallas as pl
from jax.experimental.pallas import tpu as pltpu

F32 = jnp.float32
BF16 = jnp.bfloat16

V7X_VMEM_LIMIT_BYTES = 56 * 1024 * 1024
LANES = 128

RMS_EPS = 1e-6
CONV_K = 5
N_MIXERS = 4

SSD_HEAD_DIM = 64
SSD_N_GROUPS = 8
SSD_HEADS_PER_GROUP = 4
SSD_D_STATE = 128
SSD_CHUNK = 128
SSD_D_INNER = 2048

GDN_N_K_HEADS = 8
GDN_N_V_HEADS = 16
GDN_HEAD = 128
GDN_CHUNK = 64
GDN_QK_DIM = 1024
GDN_V_DIM = 2048

ATT_N_HEADS = 16
ATT_N_KV_HEADS = 4
ATT_HEAD_DIM = 64
ATT_WINDOW = 128
ROPE_THETA = 10000.0
ATT_BLOCKS_PER_STEP = 4

RWKV_HEAD = 64
RWKV_CHUNK = 64
RWKV_LN_EPS = 64e-5

PROJ_AUX = 256
PROJ_PAD = 6144 + PROJ_AUX
PROJ_CHUNK = 512

TOKEN_TILE = 512
FFN_TOKEN_TILE = 1024
FFN_CHUNK = 256
FACTOR_CHUNKS = 8


def _cparams(*sem):
    return pltpu.CompilerParams(dimension_semantics=sem, vmem_limit_bytes=V7X_VMEM_LIMIT_BYTES)


def _resident(shape):
    nd = len(shape)
    return pl.BlockSpec(shape, lambda *_: (0,) * nd, pipeline_mode=pl.Buffered(1))


def _sigmoid(x):
    return 1.0 / (1.0 + jnp.exp(-x))


def _silu(x):
    return x * _sigmoid(x)


def _softplus(x):
    return jnp.maximum(x, 0.0) + jnp.log1p(jnp.exp(-jnp.abs(x)))


def _rms(x, g):
    return x * lax.rsqrt(jnp.mean(x * x, axis=-1, keepdims=True) + RMS_EPS) * g


def _dot(a, b):
    return jnp.dot(a.astype(BF16), b.astype(BF16), preferred_element_type=F32)


def _dot_nt(a, b):
    return lax.dot_general(a.astype(BF16), b.astype(BF16), (((1,), (1,)), ((), ())), preferred_element_type=F32)


def _dot_f32(mask, x):
    m = mask.astype(BF16)
    hi = x.astype(BF16)
    rest = x - hi.astype(F32)
    mid = rest.astype(BF16)
    lo = (rest - mid.astype(F32)).astype(BF16)
    return (jnp.dot(m, hi, preferred_element_type=F32) + jnp.dot(m, mid, preferred_element_type=F32)
            + jnp.dot(m, lo, preferred_element_type=F32))


def _tri_apply_many(ns, xs, size):
    steps = int(math.log2(size))
    ms = list(ns)
    xs = list(xs)
    for s in range(steps):
        prods = [_dot(m, x) for m, x in zip(ms, xs)]
        if s + 1 < steps:
            ms = [_dot(m, m) for m in ms]
        xs = [x + p for x, p in zip(xs, prods)]
    return xs


def _seg_sum2(x):
    lane = lax.broadcasted_iota(jnp.int32, x.shape, 1)
    lo = lane < 64
    s_lo = jnp.sum(jnp.where(lo, x, 0.0), axis=1, keepdims=True)
    s_hi = jnp.sum(jnp.where(lo, 0.0, x), axis=1, keepdims=True)
    return jnp.where(lo, s_lo, s_hi)


def _tri_masks(n, forward):
    li = lax.broadcasted_iota(jnp.int32, (n, n), 0)
    si = lax.broadcasted_iota(jnp.int32, (n, n), 1)
    if forward:
        return li >= si, li > si
    return li <= si, li < si


def _ffn_body(x_ref, g_ref, wg_ref, wu_ref, wd_ref, fg_ref, o_ref, xn_ref, acc_ref, *, n_chunks, final):
    x = x_ref[...]
    xn_ref[...] = _rms(x, g_ref[...]).astype(BF16)

    def down(c):
        xb = xn_ref[...]
        gate = jnp.dot(xb, wg_ref[c], preferred_element_type=F32)
        up = jnp.dot(xb, wu_ref[c], preferred_element_type=F32)
        h = (_silu(gate) * up).astype(BF16)
        return jnp.dot(h, wd_ref[c], preferred_element_type=F32)

    def chunk(c, carry):
        acc_ref[...] += down(c)
        return carry

    acc_ref[...] = down(0)
    lax.fori_loop(1, n_chunks, chunk, 0)
    y = x + 0.5 * acc_ref[...]
    if final:
        y = _rms(y, fg_ref[...])
    o_ref[...] = y


def _ffn(x, g, wg3, wu3, wd3, final_g, final):
    t, d = x.shape
    tm = FFN_TOKEN_TILE
    body = functools.partial(_ffn_body, n_chunks=wg3.shape[0], final=final)
    return pl.pallas_call(
        body,
        grid=(t // tm,),
        in_specs=[pl.BlockSpec((tm, d), lambda i: (i, 0)),
                  _resident((1, d)), _resident(wg3.shape), _resident(wu3.shape), _resident(wd3.shape),
                  _resident((1, d))],
        out_specs=pl.BlockSpec((tm, d), lambda i: (i, 0)),
        out_shape=jax.ShapeDtypeStruct((t, d), F32),
        scratch_shapes=[pltpu.VMEM((tm, d), BF16), pltpu.VMEM((tm, d), F32)],
        compiler_params=_cparams("parallel"),
        name="ffn",
    )(x, g, wg3, wu3, wd3, final_g)


def _proj_body(x_ref, g_ref, w_ref, *out_refs, n_main):
    xn = _rms(x_ref[...], g_ref[...]).astype(BF16)
    main_ref = out_refs[0]
    for j in range(0, n_main, PROJ_CHUNK):
        main_ref[:, j:j + PROJ_CHUNK] = jnp.dot(xn, w_ref[:, j:j + PROJ_CHUNK],
                                                preferred_element_type=F32).astype(BF16)
    if len(out_refs) > 1:
        out_refs[1][...] = jnp.dot(xn, w_ref[:, n_main:], preferred_element_type=F32)


def _proj(x, g, w, n_aux):
    t, d = x.shape
    n_main = w.shape[1] - n_aux
    tm = TOKEN_TILE
    out_specs = [pl.BlockSpec((tm, n_main), lambda i: (i, 0))]
    out_shape = [jax.ShapeDtypeStruct((t, n_main), BF16)]
    if n_aux:
        out_specs.append(pl.BlockSpec((tm, n_aux), lambda i: (i, 0)))
        out_shape.append(jax.ShapeDtypeStruct((t, n_aux), F32))
    return pl.pallas_call(
        functools.partial(_proj_body, n_main=n_main),
        grid=(t // tm,),
        in_specs=[pl.BlockSpec((tm, d), lambda i: (i, 0)), _resident((1, d)), _resident(w.shape)],
        out_specs=out_specs,
        out_shape=out_shape,
        compiler_params=_cparams("parallel"),
        name="norm_proj",
    )(x, g, w)


def _proj_conv_body(x_ref, xp_ref, xn_ref, g_ref, w_ref, cw_ref, cb_ref, main_ref, aux_ref,
                    *, n_tiles, conv0, conv_n, l2_n, scaled_n, n_main):
    i = pl.program_id(1)
    g = g_ref[...]
    tl = x_ref.shape[0]
    halo = xp_ref.shape[0]
    xm = _rms(x_ref[...], g)
    xn = xm.astype(BF16)
    xe = jnp.concatenate([jnp.where(i > 0, _rms(xp_ref[...], g), 0.0), xm,
                          jnp.where(i < n_tiles - 1, _rms(xn_ref[...], g), 0.0)], axis=0).astype(BF16)
    rows = tl + 2 * halo
    half = CONV_K // 2
    for j in range(0, n_main, PROJ_CHUNK):
        cols = slice(j, j + PROJ_CHUNK)
        if not conv0 <= j < conv0 + conv_n:
            main_ref[:, cols] = jnp.dot(xn, w_ref[:, cols], preferred_element_type=F32).astype(BF16)
            continue
        jc = j - conv0
        pe = jnp.dot(xe, w_ref[:, cols], preferred_element_type=F32)
        cw = cw_ref[:, jc:jc + PROJ_CHUNK]
        acc = pe * cw[half:half + 1, :] + cb_ref[:, jc:jc + PROJ_CHUNK]
        for k in range(-half, half + 1):
            if k:
                acc = acc + pltpu.roll(pe, (-k) % rows, axis=0) * cw[k + half:k + half + 1, :]
        y = _silu(acc[halo:halo + tl, :])
        if jc < l2_n:
            scale = GDN_HEAD ** -0.5 if jc < scaled_n else 1.0
            y = jnp.concatenate(
                [y[:, h:h + LANES] * (lax.rsqrt(jnp.sum(y[:, h:h + LANES] * y[:, h:h + LANES], axis=-1,
                                                         keepdims=True) + 1e-6) * scale)
                 for h in range(0, PROJ_CHUNK, LANES)], axis=1)
        main_ref[:, cols] = y.astype(BF16)
    aux_ref[...] = jnp.dot(xn, w_ref[:, n_main:], preferred_element_type=F32)


def _proj_conv(x3, g, w, conv_w, conv_b, conv0, l2_n, scaled_n):
    b, l, d = x3.shape
    tl = TOKEN_TILE
    halo = 8
    nt = l // tl
    n_main = w.shape[1] - PROJ_AUX
    x4 = x3.reshape(b, l // halo, halo, d)
    body = functools.partial(_proj_conv_body, n_tiles=nt, conv0=conv0, conv_n=conv_w.shape[1], l2_n=l2_n,
                             scaled_n=scaled_n, n_main=n_main)
    return pl.pallas_call(
        body,
        grid=(b, nt),
        in_specs=[pl.BlockSpec((None, tl, d), lambda i, j: (i, j, 0)),
                  pl.BlockSpec((None, None, halo, d),
                               lambda i, j: (i, jnp.maximum(j * (tl // halo) - 1, 0), 0, 0)),
                  pl.BlockSpec((None, None, halo, d),
                               lambda i, j: (i, jnp.minimum((j + 1) * (tl // halo), l // halo - 1), 0, 0)),
                  _resident((1, d)), _resident(w.shape), _resident(conv_w.shape), _resident(conv_b.shape)],
        out_specs=[pl.BlockSpec((None, tl, n_main), lambda i, j: (i, j, 0)),
                   pl.BlockSpec((None, tl, PROJ_AUX), lambda i, j: (i, j, 0))],
        out_shape=[jax.ShapeDtypeStruct((b, l, n_main), BF16), jax.ShapeDtypeStruct((b, l, PROJ_AUX), F32)],
        compiler_params=_cparams("parallel", "parallel"),
        name="proj_conv",
    )(x3, x4, x4, g, w, conv_w, conv_b)


def _ssd_prep_body(p_ref, bias_ref, alog_ref, dt_ref, cs_ref, *, q):
    raw = p_ref[:, 0:LANES]
    dt = _softplus(raw + bias_ref[...])
    dta = dt * (-jnp.exp(alog_ref[...]))
    dt_ref[...] = dt
    incl_f, _ = _tri_masks(q, True)
    incl_b, _ = _tri_masks(q, False)
    tril = incl_f.astype(F32)
    triu = incl_b.astype(F32)
    lane = lax.broadcasted_iota(jnp.int32, (q, LANES), 1)
    n_heads = SSD_N_GROUPS * SSD_HEADS_PER_GROUP
    for c in range(raw.shape[0] // q):
        blk = dta[c * q:(c + 1) * q, :]
        cs_ref[c * q:(c + 1) * q, :] = jnp.where(lane < n_heads, _dot_f32(tril, blk), _dot_f32(triu, blk))


def _ssd_prep(p, bias, alog):
    t = p.shape[0]
    tm = TOKEN_TILE
    blk = pl.BlockSpec((tm, LANES), lambda i: (i, 0))
    return pl.pallas_call(
        functools.partial(_ssd_prep_body, q=SSD_CHUNK),
        grid=(t // tm,),
        in_specs=[pl.BlockSpec((tm, PROJ_AUX), lambda i: (i, 0)),
                  _resident((1, LANES)), _resident((1, LANES))],
        out_specs=[blk, blk],
        out_shape=[jax.ShapeDtypeStruct((t, LANES), F32)] * 2,
        compiler_params=_cparams("parallel"),
        name="ssd_prep",
    )(p, bias, alog)


def _ssd_core_body(xs_s, bm_ref, c_s, dsk_ref, dtr_ref, csr_ref, csc_ref, o_ref, b_s, yf_s, yb_s, st_ref,
                   *, q, n_chunks):
    hd, nh = SSD_HEAD_DIM, SSD_HEADS_PER_GROUP
    b_s[...] = bm_ref[...].astype(F32)
    st_ref[...] = jnp.zeros_like(st_ref)
    sh = int(math.log2(hd))
    lane_head = jnp.right_shift(lax.broadcasted_iota(jnp.int32, (q, nh * hd), 1), sh)
    lane_head_row = jnp.right_shift(lax.broadcasted_iota(jnp.int32, (1, nh * hd), 1), sh)

    def per_head(vals, lane_map):
        out = vals[nh - 1]
        for h in range(nh - 2, -1, -1):
            out = jnp.where(lane_map == h, vals[h], out)
        return out

    def step(i, carry):
        prep = []
        for d, forward in enumerate((True, False)):
            c = i if forward else n_chunks - 1 - i
            r0 = pl.multiple_of(c * q, q)
            incl, _ = _tri_masks(q, forward)
            cc = c_s[pl.ds(r0, q), :]
            bc = b_s[pl.ds(r0, q), :]
            cb = _dot_nt(cc, bc)
            bct = bc.T
            xc = xs_s[pl.ds(r0, q), :].astype(BF16)
            acols = csc_ref[pl.ds(r0, q), :]
            arows = csr_ref[c]
            dtrows = dtr_ref[c]
            lmats, wmats, eas, tots = [], [], [], []
            for h in range(nh):
                k = d * nh + h
                acol = jnp.broadcast_to(acols[:, k:k + 1], (q, q))
                arow = arows[k:k + 1, :]
                dtr = dtrows[k:k + 1, :]
                tot = arow[:, q - 1:q] if forward else arow[:, 0:1]
                lmats.append((jnp.where(incl, jnp.exp(acol - arow), 0.0) * cb * dtr).astype(BF16))
                wmats.append((bct * (jnp.exp(tot - arow) * dtr)).astype(BF16))
                eas.append(jnp.exp(jnp.concatenate([acol] * (nh * hd // q), axis=1)))
                tots.append(jnp.exp(tot))
            xms = [jnp.where(lane_head == h, xc, jnp.zeros_like(xc)) for h in range(nh)]
            prep.append((r0, cc, lmats, wmats, xms, per_head(eas, lane_head), per_head(tots, lane_head_row)))
        states = [st_ref[d] for d in range(2)]
        y_off = [jnp.dot(p[1], s.astype(BF16), preferred_element_type=F32) for p, s in zip(prep, states)]
        both = [sum(jnp.dot(jnp.concatenate([lm, wm], axis=0), xm, preferred_element_type=F32)
                    for lm, wm, xm in zip(p[2], p[3], p[4])) for p in prep]
        for d, y_s in enumerate((yf_s, yb_s)):
            y_s[pl.ds(prep[d][0], q), :] = both[d][:q, :] + y_off[d] * prep[d][5]
            st_ref[d] = states[d] * prep[d][6] + both[d][q:, :]
        return carry

    lax.fori_loop(0, n_chunks, step, 0)
    o_ref[...] = yf_s[...] + yb_s[...] + xs_s[...].astype(F32) * dsk_ref[...]


def _ssd_core(p3, dskip, dt_rows, cs_rows, cs_cols):
    b, l, _ = p3.shape
    q = SSD_CHUNK
    nc = l // q
    gw = SSD_HEADS_PER_GROUP * SSD_HEAD_DIM
    ns = SSD_D_STATE
    x0 = SSD_D_INNER // gw
    b0 = 2 * SSD_D_INNER // ns
    c0 = b0 + SSD_N_GROUPS
    rows_spec = pl.BlockSpec((None, None, nc, 8, q), lambda i, g: (i, g, 0, 0, 0))
    return pl.pallas_call(
        functools.partial(_ssd_core_body, q=q, n_chunks=nc),
        grid=(b, SSD_N_GROUPS),
        in_specs=[pl.BlockSpec((None, l, gw), lambda i, g: (i, 0, x0 + g)),
                  pl.BlockSpec((None, l, ns), lambda i, g: (i, 0, b0 + g)),
                  pl.BlockSpec((None, l, ns), lambda i, g: (i, 0, c0 + g)),
                  pl.BlockSpec((1, gw), lambda i, g: (0, g)),
                  rows_spec, rows_spec,
                  pl.BlockSpec((None, None, l, 8), lambda i, g: (i, g, 0, 0))],
        out_specs=pl.BlockSpec((None, l, gw), lambda i, g: (i, 0, g)),
        out_shape=jax.ShapeDtypeStruct((b, l, SSD_D_INNER), F32),
        scratch_shapes=[pltpu.VMEM((l, ns), F32), pltpu.VMEM((l, gw), F32), pltpu.VMEM((l, gw), F32),
                        pltpu.VMEM((2, ns, gw), F32)],
        compiler_params=_cparams("parallel", "parallel"),
        name="ssd_core",
    )(p3, p3, p3, dskip, dt_rows, cs_rows, cs_cols)


def _ssd_out_body(y_ref, z_ref, nw_ref, w_ref, x_ref, o_ref):
    yz = y_ref[...] * _silu(z_ref[...].astype(F32))
    o_ref[...] = x_ref[...] + jnp.dot(_rms(yz, nw_ref[...]).astype(BF16), w_ref[...], preferred_element_type=F32)


def _ssd_out(y, p, norm_w, w_out, x):
    t, d = x.shape
    tm = TOKEN_TILE
    di = SSD_D_INNER
    return pl.pallas_call(
        _ssd_out_body,
        grid=(t // tm,),
        in_specs=[pl.BlockSpec((tm, di), lambda i: (i, 0)),
                  pl.BlockSpec((tm, di), lambda i: (i, 0)),
                  _resident((1, di)), _resident(w_out.shape),
                  pl.BlockSpec((tm, d), lambda i: (i, 0))],
        out_specs=pl.BlockSpec((tm, d), lambda i: (i, 0)),
        out_shape=jax.ShapeDtypeStruct((t, d), F32),
        compiler_params=_cparams("parallel"),
        name="ssd_out",
    )(y, p, norm_w, w_out, x)


def _group_rows_cols(a, b, l, nc, q):
    g = a.shape[2]
    cols = jnp.transpose(a, (0, 2, 1, 3))
    rows = jnp.transpose(a.reshape(b, nc, q, g, 8), (0, 3, 1, 4, 2))
    return rows, cols


def _ssd_mixer(x, b, l, g_norm, w_in, conv_w, conv_b, a_log, dt_bias, d_skip, norm_w, w_out):
    t = b * l
    q = SSD_CHUNK
    nc = l // q
    p3, aux = _proj_conv(x.reshape(b, l, -1), g_norm, w_in, conv_w, conv_b, SSD_D_INNER, 0, 0)
    p = p3.reshape(t, -1)
    dt, cs = _ssd_prep(aux.reshape(t, -1), dt_bias, a_log)
    ng, nh = SSD_N_GROUPS, SSD_HEADS_PER_GROUP

    def regroup(a):
        a = a[:, :2 * ng * nh].reshape(b, l, 2, ng, nh)
        return jnp.transpose(a, (0, 1, 3, 2, 4)).reshape(b, l, ng, 2 * nh)

    dt_rows, _ = _group_rows_cols(regroup(dt), b, l, nc, q)
    cs_rows, cs_cols = _group_rows_cols(regroup(cs), b, l, nc, q)
    y = _ssd_core(p3, d_skip, dt_rows, cs_rows, cs_cols)
    return _ssd_out(y.reshape(t, SSD_D_INNER), p, norm_w, w_out, x)


def _gdn_prep_body(p_ref, bias_ref, alog_ref, o_ref, *, cs):
    raw = p_ref[:, 0:LANES]
    beta = _sigmoid(raw)
    g = -jnp.exp(alog_ref[...]) * _softplus(raw + bias_ref[...])
    n = LANES
    li = lax.broadcasted_iota(jnp.int32, (n, n), 0)
    si = lax.broadcasted_iota(jnp.int32, (n, n), 1)
    sh = int(math.log2(cs))
    same = jnp.right_shift(li, sh) == jnp.right_shift(si, sh)
    tril = (same & (li >= si)).astype(F32)
    triu = (same & (li <= si)).astype(F32)
    lane = lax.broadcasted_iota(jnp.int32, (n, LANES), 1)
    hv = GDN_N_V_HEADS
    for c in range(raw.shape[0] // n):
        blk = g[c * n:(c + 1) * n, :]
        o_ref[c * n:(c + 1) * n, :] = jnp.where(
            lane < hv, beta[c * n:(c + 1) * n, :],
            jnp.where(lane < 2 * hv, _dot_f32(tril, blk), _dot_f32(triu, blk)))


def _gdn_prep(p, bias, alog):
    t = p.shape[0]
    tm = TOKEN_TILE
    return pl.pallas_call(
        functools.partial(_gdn_prep_body, cs=GDN_CHUNK),
        grid=(t // tm,),
        in_specs=[pl.BlockSpec((tm, PROJ_AUX), lambda i: (i, 0)),
                  _resident((1, LANES)), _resident((1, LANES))],
        out_specs=pl.BlockSpec((tm, LANES), lambda i: (i, 0)),
        out_shape=jax.ShapeDtypeStruct((t, LANES), F32),
        compiler_params=_cparams("parallel"),
        name="gdn_prep",
    )(p, bias, alog)


def _gdn_core_body(q_ref, k_ref, v_ref, grow_ref, cols_ref,
                   o_ref, q_s, k_s, v_s, mg_s, c_s, d_s, yf_s, yb_s, st_ref, *, cs, n_chunks):
    hd = GDN_HEAD
    q_s[...] = q_ref[...].astype(F32)
    k_s[...] = k_ref[...].astype(F32)
    v_s[...] = v_ref[...].astype(F32)
    st_ref[...] = jnp.zeros_like(st_ref)

    lane_blk = jnp.right_shift(lax.broadcasted_iota(jnp.int32, (cs, 4 * cs), 1), int(math.log2(cs)))
    t_row = lax.broadcasted_iota(jnp.int32, (cs, 4 * cs), 0)
    t_col = jnp.bitwise_and(lax.broadcasted_iota(jnp.int32, (cs, 4 * cs), 1), cs - 1)
    eye_packed = jnp.where(t_row == t_col, 1.0, 0.0)
    fwd_blk = lane_blk < 2
    incl_p = (fwd_blk & (t_row >= t_col)) | (jnp.logical_not(fwd_blk) & (t_row <= t_col))
    strict_p = (fwd_blk & (t_row > t_col)) | (jnp.logical_not(fwd_blk) & (t_row < t_col))
    head0_row = lax.broadcasted_iota(jnp.int32, (1, 2 * hd), 1) < hd
    half_row = lax.broadcasted_iota(jnp.int32, (1, 2 * cs), 1) < cs
    half3 = lax.broadcasted_iota(jnp.int32, (3 * cs, 2 * cs), 1) < cs

    def block_diag(z):
        return jnp.concatenate([jnp.where(lane_blk == r, z, 0.0) for r in range(4)], axis=0).astype(BF16)

    def pick(cols, idx):
        out = jnp.where(lane_blk == 2, cols[:, idx[2]:idx[2] + 1], cols[:, idx[3]:idx[3] + 1])
        out = jnp.where(lane_blk == 1, cols[:, idx[1]:idx[1] + 1], out)
        return jnp.where(lane_blk == 0, cols[:, idx[0]:idx[0] + 1], out)

    def chunk_end(g, forward):
        return (g[:, cs - 1:cs], g[:, 2 * cs - 1:2 * cs]) if forward else (g[:, 0:1], g[:, cs:cs + 1])

    def factor(i, carry):
        packs = []
        for j in range(FACTOR_CHUNKS):
            c = i * FACTOR_CHUNKS + j
            r0 = pl.multiple_of(c * cs, cs)
            c3 = pl.multiple_of(c * 3 * cs, cs)
            c4 = pl.multiple_of(c * 4 * cs, 4 * cs)
            qc = q_s[pl.ds(r0, cs), :]
            kc = k_s[pl.ds(r0, cs), :]
            vc = v_s[pl.ds(r0, cs), :]
            k4 = jnp.concatenate([kc] * 4, axis=0)
            packs.append((r0, c3, c4, qc, kc, vc, cols_ref[pl.ds(r0, cs), :], grow_ref[c],
                          _dot_nt(kc, k4), _dot_nt(qc, k4)))
        n_p, rhs_all, lhs_all, qd_all = [], [], [], []
        for r0, c3, c4, qc, kc, vc, cols, grow_p, kk4, qk4 in packs:
            dec_p = jnp.where(incl_p, jnp.exp(pick(cols, (2, 3, 4, 5)) - grow_p), 0.0)
            n_p.append(jnp.where(strict_p, -(pick(cols, (0, 1, 0, 1)) * kk4 * dec_p), 0.0))
            qkm = jnp.where(incl_p, qk4 * dec_p, 0.0)
            kt2 = jnp.concatenate([kc, kc], axis=0).T
            rhs, lhs, qds = [], [], []
            for d, forward in enumerate((True, False)):
                g = grow_p[:, d * 2 * cs:(d + 1) * 2 * cs]
                gl0, gl1 = chunk_end(g, forward)
                pair = jnp.concatenate([kt2 * jnp.exp(jnp.where(half_row, gl0, gl1) - g),
                                        qkm[:, d * 2 * cs:(d + 1) * 2 * cs]], axis=0)
                for e in range(2):
                    bcol = cols[:, e:e + 1]
                    eg = jnp.exp(cols[:, 2 + 2 * d + e:3 + 2 * d + e])
                    rhs.append(jnp.concatenate([vc[:, e * hd:(e + 1) * hd] * bcol, kc * (bcol * eg)],
                                               axis=1).astype(BF16))
                    lhs.append(jnp.where(half3 if e == 0 else jnp.logical_not(half3), pair, 0.0).astype(BF16))
                    qds.append(qc * eg)
            rhs_all.append(rhs)
            lhs_all.append(lhs)
            qd_all.append(qds)
        m_p = n_p
        inv_p = [eye_packed + m for m in n_p]
        steps = int(math.log2(cs))
        for s in range(1, steps):
            bds = [block_diag(m) for m in m_p]
            if s == 1:
                m_p = [jnp.dot(m.astype(BF16), bd, preferred_element_type=F32) for m, bd in zip(m_p, bds)]
                continue
            res = [jnp.dot(jnp.concatenate([p, m], axis=0).astype(BF16), bd, preferred_element_type=F32)
                   for p, m, bd in zip(inv_p, m_p, bds)]
            inv_p = [p + r[:cs, :] for p, r in zip(inv_p, res)]
            m_p = [r[cs:, :] for r in res]
        bds = [block_diag(m) for m in m_p]
        inv_p = [(p + jnp.dot(p.astype(BF16), bd, preferred_element_type=F32)).astype(BF16)
                 for p, bd in zip(inv_p, bds)]
        zero = jnp.zeros((cs, 2 * hd), BF16)
        sols = [[jnp.dot(p, jnp.concatenate([zero] * r + [rhs[r]] + [zero] * (3 - r), axis=0),
                         preferred_element_type=F32) for r in range(4)] for p, rhs in zip(inv_p, rhs_all)]
        folded = [[jnp.dot(lh[2 * d + e],
                           jnp.concatenate([sol[2 * d], sol[2 * d + 1]], axis=0).astype(BF16),
                           preferred_element_type=F32) for d in range(2) for e in range(2)]
                  for lh, sol in zip(lhs_all, sols)]
        for (r0, c3, c4, *_), fold, qds in zip(packs, folded, qd_all):
            c6 = pl.multiple_of(c3 * 2, 2 * cs)
            c2 = pl.multiple_of(r0 * 2, 2 * cs)
            for d in range(2):
                f0, f1 = fold[2 * d], fold[2 * d + 1]
                mg_s[d, pl.ds(c6, 6 * cs), :] = jnp.concatenate(
                    [-f0[:2 * cs, hd:], qds[2 * d] - f0[2 * cs:, hd:],
                     -f1[:2 * cs, hd:], qds[2 * d + 1] - f1[2 * cs:, hd:]], axis=0).astype(BF16)
                c_s[d, pl.ds(c2, 2 * cs), :] = jnp.concatenate([f0[:2 * cs, :hd], f1[:2 * cs, :hd]], axis=1)
                d_s[d, pl.ds(r0, cs), :] = jnp.concatenate([f0[2 * cs:, :hd], f1[2 * cs:, :hd]], axis=1)
        return carry

    def scan(i, carry):
        cidx = [i, n_chunks - 1 - i]
        states = [st_ref[d] for d in range(2)]
        res = [[jnp.dot(mg_s[d, pl.ds(pl.multiple_of(cidx[d] * 6 * cs, 2 * cs) + 3 * e * cs, 3 * cs), :],
                        states[d][:, e * hd:(e + 1) * hd].astype(BF16), preferred_element_type=F32)
                for e in range(2)] for d in range(2)]
        for d, (forward, y_s) in enumerate(((True, yf_s), (False, yb_s))):
            c = cidx[d]
            gl0, gl1 = chunk_end(grow_ref[c][:, d * 2 * cs:(d + 1) * 2 * cs], forward)
            r0, r1 = res[d]
            y_s[pl.ds(pl.multiple_of(c * cs, cs), cs), :] = (
                jnp.concatenate([r0[2 * cs:, :], r1[2 * cs:, :]], axis=1)
                + d_s[d, pl.ds(pl.multiple_of(c * cs, cs), cs), :])
            st_ref[d] = (states[d] * jnp.exp(jnp.where(head0_row, gl0, gl1))
                         + jnp.concatenate([r0[:2 * cs, :], r1[:2 * cs, :]], axis=1)
                         + c_s[d, pl.ds(pl.multiple_of(c * 2 * cs, 2 * cs), 2 * cs), :])
        return carry

    lax.fori_loop(0, n_chunks // FACTOR_CHUNKS, factor, 0)
    lax.fori_loop(0, n_chunks, scan, 0)
    o_ref[...] = yf_s[...] + yb_s[...]


def _gdn_core(p3, grow4, cols):
    b, l, _ = p3.shape
    cs = GDN_CHUNK
    nc = l // cs
    hd = GDN_HEAD
    vw = 2 * hd
    k0 = GDN_QK_DIM // hd
    v0 = 2 * GDN_QK_DIM // vw
    assert l % (cs * FACTOR_CHUNKS) == 0, (l, cs, FACTOR_CHUNKS)
    return pl.pallas_call(
        functools.partial(_gdn_core_body, cs=cs, n_chunks=nc),
        grid=(b, GDN_N_K_HEADS),
        in_specs=[pl.BlockSpec((None, l, hd), lambda i, j: (i, 0, j)),
                  pl.BlockSpec((None, l, hd), lambda i, j: (i, 0, k0 + j)),
                  pl.BlockSpec((None, l, vw), lambda i, j: (i, 0, v0 + j)),
                  pl.BlockSpec((None, None, nc, 1, 4 * cs), lambda i, j: (i, j, 0, 0, 0)),
                  pl.BlockSpec((None, None, l, 8), lambda i, j: (i, j, 0, 0))],
        out_specs=pl.BlockSpec((None, l, vw), lambda i, j: (i, 0, j)),
        out_shape=jax.ShapeDtypeStruct((b, l, GDN_V_DIM), F32),
        scratch_shapes=[pltpu.VMEM((l, hd), F32), pltpu.VMEM((l, hd), F32), pltpu.VMEM((l, vw), F32),
                        pltpu.VMEM((2, 6 * l, hd), BF16),
                        pltpu.VMEM((2, 2 * l, vw), F32),
                        pltpu.VMEM((2, l, vw), F32),
                        pltpu.VMEM((l, vw), F32), pltpu.VMEM((l, vw), F32), pltpu.VMEM((2, hd, vw), F32)],
        compiler_params=_cparams("parallel", "parallel"),
        name="gdn_core",
    )(p3, p3, p3, grow4, cols)


def _gdn_out_body(o_ref, z_ref, nw_ref, w_ref, x_ref, out_ref, h_s):
    hd = GDN_HEAD
    nw = nw_ref[...]
    for e in range(GDN_N_V_HEADS):
        oe = o_ref[:, e * hd:(e + 1) * hd]
        ze = z_ref[:, e * hd:(e + 1) * hd].astype(F32)
        h_s[:, e * hd:(e + 1) * hd] = (_rms(oe, nw) * _silu(ze)).astype(BF16)
    out_ref[...] = x_ref[...] + jnp.dot(h_s[...], w_ref[...], preferred_element_type=F32)


def _gdn_out(o, p, norm_w, w_out, x):
    t, d = x.shape
    tm = TOKEN_TILE
    dv = GDN_V_DIM
    return pl.pallas_call(
        _gdn_out_body,
        grid=(t // tm,),
        in_specs=[pl.BlockSpec((tm, dv), lambda i: (i, 0)),
                  pl.BlockSpec((tm, dv), lambda i: (i, 2 * GDN_QK_DIM // dv + 1)),
                  _resident((1, GDN_HEAD)), _resident(w_out.shape),
                  pl.BlockSpec((tm, d), lambda i: (i, 0))],
        out_specs=pl.BlockSpec((tm, d), lambda i: (i, 0)),
        out_shape=jax.ShapeDtypeStruct((t, d), F32),
        scratch_shapes=[pltpu.VMEM((tm, dv), BF16)],
        compiler_params=_cparams("parallel"),
        name="gdn_out",
    )(o, p, norm_w, w_out, x)


def _gdn_mixer(x, b, l, g_norm, w_in, conv_w, conv_b, a_log, dt_bias, norm_w, w_out):
    t = b * l
    cs = GDN_CHUNK
    nc = l // cs
    p3, aux = _proj_conv(x.reshape(b, l, -1), g_norm, w_in, conv_w, conv_b, 0, 2 * GDN_QK_DIM, GDN_QK_DIM)
    p = p3.reshape(t, -1)
    gt = _gdn_prep(aux.reshape(t, -1), dt_bias, a_log).reshape(b, l, LANES)
    hk, hv = GDN_N_K_HEADS, GDN_N_V_HEADS
    parts = [gt[..., i * hv:(i + 1) * hv].reshape(b, l, hk, 2) for i in range(3)]
    a = jnp.concatenate(parts + [jnp.zeros((b, l, hk, 2), F32)], axis=-1)
    cols = jnp.transpose(a, (0, 2, 1, 3))
    grow4 = jnp.transpose(a[..., 2:6].reshape(b, nc, cs, hk, 4), (0, 3, 1, 4, 2)).reshape(b, hk, nc, 1, 4 * cs)
    o = _gdn_core(p3, grow4, cols)
    return _gdn_out(o.reshape(t, GDN_V_DIM), p, norm_w, w_out, x)


def _att_core_body(q_ref, k_ref, v_ref, cos_ref, sin_ref, sink_ref, o_ref, q_s, k_s, v_s, band_s,
                   *, win, n_blocks, seq):
    dh = ATT_HEAD_DIM
    cos = cos_ref[...]
    sin = sin_ref[...]

    def rope(x):
        w = x.shape[1]
        lane = lax.broadcasted_iota(jnp.int32, x.shape, 1)
        first = jnp.bitwise_and(lane, dh - 1) < dh // 2
        partner = jnp.where(first, pltpu.roll(x, w - dh // 2, axis=1), pltpu.roll(x, dh // 2, axis=1))
        reps = w // LANES
        c = jnp.concatenate([cos] * reps, axis=1) if reps > 1 else cos
        s = jnp.concatenate([sin] * reps, axis=1) if reps > 1 else sin
        return x * c + partner * s

    n_heads = q_ref.shape[1] // dh
    rep = ATT_N_HEADS // ATT_N_KV_HEADS
    groups = range(n_heads // rep)
    qr = rope(q_ref[...].astype(F32)) * (dh ** -0.5)
    for h in range(n_heads):
        q_s[h] = qr[:, h * dh:(h + 1) * dh].astype(BF16)
    kr = rope(k_ref[...].astype(F32))
    vv = v_ref[...]
    ones = jnp.ones((seq, dh), BF16)
    for g in groups:
        for ref in (k_s, v_s):
            ref[g, pl.ds(0, win), :] = jnp.zeros((win, ref.shape[2]), BF16)
            ref[g, pl.ds(win + seq, win), :] = jnp.zeros((win, ref.shape[2]), BF16)
        k_s[g, pl.ds(win, seq), :] = kr[:, g * dh:(g + 1) * dh].astype(BF16)
        v_s[g, pl.ds(win, seq), :] = jnp.concatenate([vv[:, g * dh:(g + 1) * dh], ones], axis=1)
    qi = lax.broadcasted_iota(jnp.int32, (win, 3 * win), 0)
    ki = lax.broadcasted_iota(jnp.int32, (win, 3 * win), 1)
    band_s[...] = jnp.where(jnp.abs(ki - win - qi) <= win, 0.0, -jnp.inf)
    sinks = sink_ref[...]

    def block_pair(n, carry):
        starts = [pl.multiple_of((ATT_BLOCKS_PER_STEP * n + j) * win, win) for j in range(ATT_BLOCKS_PER_STEP)]
        biases = []
        for r0 in starts:
            kpos = r0 - win + lax.broadcasted_iota(jnp.int32, (1, 3 * win), 1)
            biases.append(band_s[...] + jnp.where((kpos >= 0) & (kpos < seq), 0.0, -jnp.inf))
        scores = [[_dot_nt(jnp.concatenate([q_s[g * rep + r, pl.ds(r0, win), :] for r in range(rep)], axis=0),
                           k_s[g, pl.ds(r0, 3 * win), :]) for g in groups] for r0 in starts]
        exps, maxes = [], []
        for j in range(ATT_BLOCKS_PER_STEP):
            exps.append([])
            maxes.append([])
            for g in groups:
                es = []
                for r in range(rep):
                    s = scores[j][g][r * win:(r + 1) * win, :] + biases[j]
                    sink = sinks[:, g * rep + r:g * rep + r + 1]
                    m = jnp.maximum(jnp.max(s, axis=-1, keepdims=True), sink)
                    es.append(jnp.exp(s - m).astype(BF16))
                    maxes[j].append(m)
                exps[j].append(jnp.concatenate(es, axis=0))
        nds = [[jnp.dot(exps[j][g], v_s[g, pl.ds(r0, 3 * win), :], preferred_element_type=F32) for g in groups]
               for j, r0 in enumerate(starts)]
        for j, r0 in enumerate(starts):
            outs = []
            for g in groups:
                for r in range(rep):
                    h = g * rep + r
                    nd = nds[j][g][r * win:(r + 1) * win, :]
                    outs.append(nd[:, :dh] / (nd[:, dh:] + jnp.exp(sinks[:, h:h + 1] - maxes[j][h])))
            o_ref[pl.ds(r0, win), :] = jnp.concatenate(outs, axis=1).astype(o_ref.dtype)
        return carry

    lax.fori_loop(0, n_blocks // ATT_BLOCKS_PER_STEP, block_pair, 0)


def _att_core(p3, cos2, sin2, sinks3):
    b, l, _ = p3.shape
    win = ATT_WINDOW
    dh = ATT_HEAD_DIM
    steps = ATT_N_KV_HEADS * dh // LANES
    qw = ATT_N_HEADS * dh // steps
    k0 = ATT_N_HEADS * dh // LANES
    v0 = k0 + steps
    assert l % (win * ATT_BLOCKS_PER_STEP) == 0, (l, win, ATT_BLOCKS_PER_STEP)
    return pl.pallas_call(
        functools.partial(_att_core_body, win=win, n_blocks=l // win, seq=l),
        grid=(b, steps),
        in_specs=[pl.BlockSpec((None, l, qw), lambda i, j: (i, 0, j)),
                  pl.BlockSpec((None, l, LANES), lambda i, j: (i, 0, k0 + j)),
                  pl.BlockSpec((None, l, LANES), lambda i, j: (i, 0, v0 + j)),
                  _resident((l, LANES)), _resident((l, LANES)),
                  pl.BlockSpec((None, 1, qw // dh), lambda i, j: (j, 0, 0))],
        out_specs=pl.BlockSpec((None, l, qw), lambda i, j: (i, 0, j)),
        out_shape=jax.ShapeDtypeStruct((b, l, ATT_N_HEADS * dh), BF16),
        scratch_shapes=[pltpu.VMEM((qw // dh, l, dh), BF16),
                        pltpu.VMEM((LANES // dh, l + 2 * win, dh), BF16),
                        pltpu.VMEM((LANES // dh, l + 2 * win, 2 * dh), BF16),
                        pltpu.VMEM((win, 3 * win), F32)],
        compiler_params=_cparams("parallel", "parallel"),
        name="att_core",
    )(p3, p3, p3, cos2, sin2, sinks3)


def _res_proj_body(a_ref, w_ref, x_ref, o_ref):
    o_ref[...] = x_ref[...] + jnp.dot(a_ref[...], w_ref[...], preferred_element_type=F32)


def _res_proj(a, w, x):
    t, d = x.shape
    k = a.shape[1]
    tm = TOKEN_TILE
    return pl.pallas_call(
        _res_proj_body,
        grid=(t // tm,),
        in_specs=[pl.BlockSpec((tm, k), lambda i: (i, 0)), _resident(w.shape),
                  pl.BlockSpec((tm, d), lambda i: (i, 0))],
        out_specs=pl.BlockSpec((tm, d), lambda i: (i, 0)),
        out_shape=jax.ShapeDtypeStruct((t, d), F32),
        compiler_params=_cparams("parallel"),
        name="res_proj",
    )(a, w, x)


def _att_mixer(x, b, l, g_norm, w_qkv, cos2, sin2, sinks3, w_out):
    t = b * l
    (p,) = _proj(x, g_norm, w_qkv, 0)
    o = _att_core(p.reshape(b, l, -1), cos2, sin2, sinks3)
    return _res_proj(o.reshape(t, -1), w_out, x)


def _rwkv_prep_body(x_ref, xp_ref, xn_ref, g_ref, mu_ref, wr_ref, wk_ref, wv_ref, w1_ref, w2_ref, w0_ref,
                    a0_ref, a1_ref, a2_ref, g1_ref, g2_ref, kk_ref, ka_ref,
                    r_o, k_o, v_o, kk_o, a_o, lw0_o, lw1_o, gate_o, *, n_tiles):
    i = pl.program_id(1)
    g = g_ref[...]
    u = _rms(x_ref[...], g)
    tl = u.shape[0]
    prev_row = jnp.where(i > 0, _rms(xp_ref[...], g)[7:8, :], 0.0)
    next_row = jnp.where(i < n_tiles - 1, _rms(xn_ref[...], g)[0:1, :], 0.0)
    row = lax.broadcasted_iota(jnp.int32, u.shape, 0)
    u_prev = jnp.where(row == 0, prev_row, pltpu.roll(u, 1, axis=0))
    u_next = jnp.where(row == tl - 1, next_row, pltpu.roll(u, tl - 1, axis=0))
    xx = 0.5 * (u_prev + u_next) - u
    mu = mu_ref[...]

    def mix(s):
        return (u + xx * mu[s:s + 1, :]).astype(BF16)

    r_o[...] = jnp.dot(mix(0), wr_ref[...], preferred_element_type=F32)
    k = jnp.dot(mix(1), wk_ref[...], preferred_element_type=F32)
    v_o[...] = jnp.dot(mix(2), wv_ref[...], preferred_element_type=F32)
    wl = jnp.tanh(jnp.dot(mix(3), w1_ref[...], preferred_element_type=F32))
    lora = w2_ref.shape[1]
    w0 = w0_ref[...]
    for d, out in enumerate((lw0_o, lw1_o)):
        z = w0[d:d + 1, :] + _dot(wl[:, d * lora:(d + 1) * lora], w2_ref[d])
        out[...] = -math.exp(-0.5) * _sigmoid(z)
    a = _sigmoid(a0_ref[...] + _dot(jnp.dot(mix(4), a1_ref[...], preferred_element_type=F32), a2_ref[...]))
    gate_o[...] = _dot(_sigmoid(jnp.dot(mix(5), g1_ref[...], preferred_element_type=F32)), g2_ref[...])
    a_o[...] = a
    kk_o[...] = k * kk_ref[...]
    k_o[...] = k * (1.0 + (a - 1.0) * ka_ref[...])


def _rwkv_prep(x3, g, mu, wr, wk, wv, w1, w2, w0, a0, a1, a2, g1, g2, k_k, k_a):
    b, l, d = x3.shape
    tl = 256
    nt = l // tl
    x4 = x3.reshape(b, l // 8, 8, d)
    tile = pl.BlockSpec((None, tl, d), lambda i, j: (i, j, 0))
    consts = [g, mu, wr, wk, wv, w1, w2, w0, a0, a1, a2, g1, g2, k_k, k_a]
    return pl.pallas_call(
        functools.partial(_rwkv_prep_body, n_tiles=nt),
        grid=(b, nt),
        in_specs=[tile,
                  pl.BlockSpec((None, None, 8, d), lambda i, j: (i, jnp.maximum(j * (tl // 8) - 1, 0), 0, 0)),
                  pl.BlockSpec((None, None, 8, d),
                               lambda i, j: (i, jnp.minimum((j + 1) * (tl // 8), l // 8 - 1), 0, 0))]
                 + [_resident(c.shape) for c in consts],
        out_specs=[tile] * 8,
        out_shape=[jax.ShapeDtypeStruct((b, l, d), F32)] * 8,
        compiler_params=_cparams("parallel", "parallel"),
        name="rwkv_prep",
    )(x3, x4, x4, *consts)


def _rwkv_core_body(r_ref, k_ref, v_ref, kk_ref, a_ref, lw0_ref, lw1_ref, rk_ref, lnw_ref, lnb_ref, o_ref,
                    kk_s, kb_s, mg_s, cd_s, pe_s, yf_s, yb_s, st_ref, *, cs, n_chunks):
    n = RWKV_HEAD
    kk = kk_ref[...]
    kk = kk * lax.rsqrt(_seg_sum2(kk * kk) + 1e-6)
    kk_s[...] = kk
    kb_s[...] = kk * a_ref[...]
    st_ref[...] = jnp.zeros_like(st_ref)

    lane_blk = jnp.right_shift(lax.broadcasted_iota(jnp.int32, (cs, 4 * cs), 1), int(math.log2(cs)))
    t_row = lax.broadcasted_iota(jnp.int32, (cs, 4 * cs), 0)
    t_col = jnp.bitwise_and(lax.broadcasted_iota(jnp.int32, (cs, 4 * cs), 1), cs - 1)
    eye_packed = jnp.where(t_row == t_col, 1.0, 0.0)
    head0 = lax.broadcasted_iota(jnp.int32, (cs, LANES), 1) < n
    head0_rows = lax.broadcasted_iota(jnp.int32, (LANES, cs), 0) < n

    def block_diag(z):
        return jnp.concatenate([jnp.where(lane_blk == r, z, 0.0) for r in range(4)], axis=0).astype(BF16)

    def split_heads_rows(z):
        return jnp.concatenate([jnp.where(head0, z, 0.0), jnp.where(head0, 0.0, z)], axis=0)

    def factor(i, carry):
        span = FACTOR_CHUNKS * cs
        base = pl.multiple_of(i * span, span)
        rows = pl.ds(base, span)
        rc2, kc2, vc2, kkc2, kbc2 = r_ref[rows, :], k_ref[rows, :], v_ref[rows, :], kk_s[rows, :], kb_s[rows, :]
        li = lax.broadcasted_iota(jnp.int32, (2 * cs, 2 * cs), 0)
        si = lax.broadcasted_iota(jnp.int32, (2 * cs, 2 * cs), 1)
        sh = int(math.log2(cs))
        same = jnp.right_shift(li, sh) == jnp.right_shift(si, sh)
        lws, cums = [], []
        for forward, lw_ref in ((True, lw0_ref), (False, lw1_ref)):
            lws.append(lw_ref[rows, :])
            tri = (same & ((li >= si) if forward else (li <= si))).astype(F32)
            cums.append(jnp.concatenate([_dot_f32(tri, lws[-1][p * 2 * cs:(p + 1) * 2 * cs, :])
                                         for p in range(FACTOR_CHUNKS // 2)], axis=0))
        units, lhss, rhss = [], [], []
        for j in range(FACTOR_CHUNKS):
            rs = slice(j * cs, (j + 1) * cs)
            c = i * FACTOR_CHUNKS + j
            r0 = pl.multiple_of(c * cs, cs)
            c2 = pl.multiple_of(c * 2 * cs, 2 * cs)
            c3 = pl.multiple_of(c * 3 * cs, cs)
            for d, forward in enumerate((True, False)):
                lw, cum = lws[d][rs, :], cums[d][rs, :]
                tot = cum[cs - 1:cs, :] if forward else cum[0:1, :]
                e_neg = jnp.exp(-cum)
                e_end = jnp.exp(tot - cum)
                a_hat = -kkc2[rs, :] * jnp.exp(cum - lw)
                r_hat = rc2[rs, :] * jnp.exp(cum)
                b_end_t = (kbc2[rs, :] * e_end).T
                k_end_t = (kc2[rs, :] * e_end).T
                upd_lhs = jnp.concatenate(
                    [jnp.where(head0_rows, b_end_t, 0.0), jnp.where(head0_rows, 0.0, b_end_t),
                     jnp.where(head0_rows, k_end_t, 0.0), jnp.where(head0_rows, 0.0, k_end_t)], axis=1)
                p_col = jnp.broadcast_to(jnp.exp(tot), (8, LANES)).T[:, 0:1]
                pe_s[d, pl.ds(c2, 2 * cs), :] = jnp.broadcast_to(p_col, (LANES, LANES))
                lhss.append(jnp.concatenate([a_hat, r_hat], axis=0))
                rhss.append(jnp.concatenate([split_heads_rows(kbc2[rs, :] * e_neg),
                                             split_heads_rows(kc2[rs, :] * e_neg)], axis=0))
                units.append((j, d, forward, r0, c2, c3, a_hat, r_hat, upd_lhs))
        grams = [_dot_nt(lh, rh) for lh, rh in zip(lhss, rhss)]
        tops = []
        lhs_all = []
        for (j, d, forward, r0, c2, c3, _, _, upd_lhs), gram in zip(units, grams):
            strict = (t_row > t_col) if forward else (t_row < t_col)
            incl = (t_row >= t_col) if forward else (t_row <= t_col)
            tops.append(jnp.where(strict, gram[:cs, :], 0.0))
            lhs_all.append(jnp.concatenate([jnp.where(incl, gram[cs:, :], 0.0), upd_lhs], axis=0).astype(BF16))
        n_p = [jnp.concatenate([tops[2 * j][:, :LANES], tops[2 * j + 1][:, :LANES]], axis=1)
               for j in range(FACTOR_CHUNKS)]
        ak_p = [jnp.concatenate([tops[2 * j][:, LANES:], tops[2 * j + 1][:, LANES:]], axis=1)
                for j in range(FACTOR_CHUNKS)]
        a_p = [jnp.concatenate([units[2 * j][6], units[2 * j + 1][6]], axis=1) for j in range(FACTOR_CHUNKS)]
        v_bd = [block_diag(jnp.concatenate([vc2[j * cs:(j + 1) * cs, :]] * 2, axis=1))
                for j in range(FACTOR_CHUNKS)]
        av_p = [jnp.dot(ak.astype(BF16), vb, preferred_element_type=F32) for ak, vb in zip(ak_p, v_bd)]
        m_p = n_p
        inv_p = [eye_packed + m for m in n_p]
        steps = int(math.log2(cs))
        for s in range(1, steps):
            bds = [block_diag(m) for m in m_p]
            if s == 1:
                m_p = [jnp.dot(m.astype(BF16), bd, preferred_element_type=F32) for m, bd in zip(m_p, bds)]
                continue
            res = [jnp.dot(jnp.concatenate([p, m], axis=0).astype(BF16), bd, preferred_element_type=F32)
                   for p, m, bd in zip(inv_p, m_p, bds)]
            inv_p = [p + r[:cs, :] for p, r in zip(inv_p, res)]
            m_p = [r[cs:, :] for r in res]
        bds = [block_diag(m) for m in m_p]
        inv_p = [p + jnp.dot(p.astype(BF16), bd, preferred_element_type=F32) for p, bd in zip(inv_p, bds)]
        sols = [jnp.dot(p.astype(BF16), jnp.concatenate([block_diag(a), block_diag(av)], axis=1),
                        preferred_element_type=F32) for p, a, av in zip(inv_p, a_p, av_p)]
        w_rows, uv_rows = [], []
        for u, (j, d, *_rest) in enumerate(units):
            w_rows.append(split_heads_rows(sols[j][:, d * LANES:(d + 1) * LANES]).astype(BF16))
            uv_rows.append(jnp.concatenate([split_heads_rows(sols[j][:, (2 + d) * LANES:(3 + d) * LANES]),
                                            split_heads_rows(vc2[j * cs:(j + 1) * cs, :])], axis=0).astype(BF16))
        by_w = [jnp.dot(lh[:, :2 * cs], w, preferred_element_type=F32) for lh, w in zip(lhs_all, w_rows)]
        by_u = [jnp.dot(lh, uv, preferred_element_type=F32) for lh, uv in zip(lhs_all, uv_rows)]
        for (j, d, _, r0, c2, c3, _, r_hat, _), bw, bu in zip(units, by_w, by_u):
            mg_s[d, pl.ds(c3, 3 * cs), :] = jnp.concatenate([bw[cs:, :], r_hat + bw[:cs, :]], axis=0).astype(BF16)
            cd_s[d, pl.ds(c3, 3 * cs), :] = jnp.concatenate([bu[cs:, :], bu[:cs, :]], axis=0)
        return carry

    def scan(i, carry):
        cidx = [i, n_chunks - 1 - i]
        states = [st_ref[d] for d in range(2)]
        res = [jnp.dot(mg_s[d, pl.ds(pl.multiple_of(cidx[d] * 3 * cs, cs), 3 * cs), :], states[d].astype(BF16),
                       preferred_element_type=F32) for d in range(2)]
        for d, y_s in enumerate((yf_s, yb_s)):
            cd = cd_s[d, pl.ds(pl.multiple_of(cidx[d] * 3 * cs, cs), 3 * cs), :]
            decay = pe_s[d, pl.ds(pl.multiple_of(cidx[d] * 2 * cs, 2 * cs), 2 * cs), :]
            y_s[pl.ds(pl.multiple_of(cidx[d] * cs, cs), cs), :] = res[d][2 * cs:, :] + cd[2 * cs:, :]
            st_ref[d] = states[d] * decay + res[d][:2 * cs, :] + cd[:2 * cs, :]
        return carry

    lax.fori_loop(0, n_chunks // FACTOR_CHUNKS, factor, 0)
    lax.fori_loop(0, n_chunks, scan, 0)
    o = yf_s[...] + yb_s[...]
    mean = _seg_sum2(o) * (1.0 / n)
    cen = o - mean
    var = _seg_sum2(cen * cen) * (1.0 / n)
    y = cen * lax.rsqrt(var + RWKV_LN_EPS) * lnw_ref[...] + lnb_ref[...]
    bonus = _seg_sum2(r_ref[...] * k_ref[...] * rk_ref[...]) * v_ref[...]
    o_ref[...] = y + bonus


def _rwkv_core(r, k, v, kk, a, lw0, lw1, r_k, lnw, lnb):
    b, l, d = r.shape
    cs = RWKV_CHUNK
    pair = pl.BlockSpec((None, l, LANES), lambda i, j: (i, 0, j))
    vec = pl.BlockSpec((1, LANES), lambda i, j: (0, j))
    assert l % (cs * FACTOR_CHUNKS) == 0, (l, cs, FACTOR_CHUNKS)
    return pl.pallas_call(
        functools.partial(_rwkv_core_body, cs=cs, n_chunks=l // cs),
        grid=(b, d // LANES),
        in_specs=[pair] * 7 + [vec] * 3,
        out_specs=pair,
        out_shape=jax.ShapeDtypeStruct((b, l, d), F32),
        scratch_shapes=[pltpu.VMEM((l, LANES), F32), pltpu.VMEM((l, LANES), F32),
                        pltpu.VMEM((2, 3 * l, LANES), BF16),
                        pltpu.VMEM((2, 3 * l, LANES), F32),
                        pltpu.VMEM((2, 2 * l, LANES), F32),
                        pltpu.VMEM((l, LANES), F32), pltpu.VMEM((l, LANES), F32),
                        pltpu.VMEM((2, LANES, LANES), F32)],
        compiler_params=_cparams("parallel", "parallel"),
        name="rwkv_core",
    )(r, k, v, kk, a, lw0, lw1, r_k, lnw, lnb)


def _gate_proj_body(y_ref, gate_ref, w_ref, x_ref, o_ref):
    o_ref[...] = x_ref[...] + jnp.dot((y_ref[...] * gate_ref[...]).astype(BF16), w_ref[...],
                                      preferred_element_type=F32)


def _gate_proj(y, gate, w, x):
    t, d = x.shape
    tm = TOKEN_TILE
    tile = pl.BlockSpec((tm, d), lambda i: (i, 0))
    return pl.pallas_call(
        _gate_proj_body,
        grid=(t // tm,),
        in_specs=[tile, tile, _resident(w.shape), tile],
        out_specs=tile,
        out_shape=jax.ShapeDtypeStruct((t, d), F32),
        compiler_params=_cparams("parallel"),
        name="rwkv_out",
    )(y, gate, w, x)


def _rwkv_mixer(x, b, l, g_norm, mu, wr, wk, wv, w1, w2, w0, a0, a1, a2, g1, g2, k_k, k_a, r_k, lnw, lnb, w_out):
    t, d = x.shape
    r, k, v, kk, a, lw0, lw1, gate = _rwkv_prep(x.reshape(b, l, d), g_norm, mu, wr, wk, wv, w1, w2, w0,
                                                a0, a1, a2, g1, g2, k_k, k_a)
    y = _rwkv_core(r, k, v, kk, a, lw0, lw1, r_k, lnw, lnb)
    return _gate_proj(y.reshape(t, d), gate.reshape(t, d), w_out, x)


def _row(v):
    return v.reshape(1, -1).astype(F32)


def _pad_cols(w, n):
    return jnp.pad(w, ((0, 0), (0, n - w.shape[1])))


def _pad_row(v, n):
    v = v.reshape(1, -1).astype(F32)
    return jnp.pad(v, ((0, 0), (0, n - v.shape[1])))


def _ffn_weights(w_gu, w_down):
    d, f2 = w_gu.shape
    f = f2 // 2
    nck = f // FFN_CHUNK
    wg3 = jnp.transpose(w_gu[:, :f].reshape(d, nck, FFN_CHUNK), (1, 0, 2)).astype(BF16)
    wu3 = jnp.transpose(w_gu[:, f:].reshape(d, nck, FFN_CHUNK), (1, 0, 2)).astype(BF16)
    wd3 = w_down.reshape(nck, FFN_CHUNK, d).astype(BF16)
    return wg3, wu3, wd3


def _rope_tables(l):
    half = ATT_HEAD_DIM // 2
    inv_freq = ROPE_THETA ** (-jnp.arange(half, dtype=F32) / half)
    ang = jnp.arange(l).astype(F32)[:, None] * inv_freq[None, :]
    cos, sin = jnp.cos(ang), jnp.sin(ang)
    reps = LANES // ATT_HEAD_DIM
    cos2 = jnp.tile(jnp.concatenate([cos, cos], axis=1), (1, reps))
    sin2 = jnp.tile(jnp.concatenate([-sin, sin], axis=1), (1, reps))
    return cos2, sin2


def kernel(x_prompt, x_sample, ffn1_norm, ffn1_w_gu, ffn1_w_down, mix_norm, ffn2_norm, ffn2_w_gu, ffn2_w_down, ssd_w_in, ssd_conv_w, ssd_conv_b, ssd_a_log, ssd_dt_bias, ssd_d, ssd_norm, ssd_w_out, gdn_w_in, gdn_conv_w, gdn_conv_b, gdn_a_log, gdn_dt_bias, gdn_norm, gdn_w_out, att_w_qkv, att_sinks, att_w_out, rwkv_x_mu, rwkv_w_rkv, rwkv_w0, rwkv_w1, rwkv_w2, rwkv_a0, rwkv_a1, rwkv_a2, rwkv_g1, rwkv_g2, rwkv_k_k, rwkv_k_a, rwkv_r_k, rwkv_lnx_w, rwkv_lnx_b, rwkv_w_out, final_norm):
    depth = ffn1_norm.shape[0]
    ffn1 = [_ffn_weights(ffn1_w_gu[i], ffn1_w_down[i]) for i in range(depth)]
    ffn2 = [_ffn_weights(ffn2_w_gu[i], ffn2_w_down[i]) for i in range(depth)]
    fin_g = _row(final_norm)
    hv = GDN_N_V_HEADS
    att_steps = ATT_N_KV_HEADS * ATT_HEAD_DIM // LANES

    def mixer(h, b, l, i):
        m, j = i % N_MIXERS, i // N_MIXERS
        g = _row(mix_norm[i])
        if m == 0:
            return _ssd_mixer(h, b, l, g, _pad_cols(ssd_w_in[j], PROJ_PAD).astype(BF16), ssd_conv_w[j],
                              _row(ssd_conv_b[j]), _pad_row(ssd_a_log[j], LANES), _pad_row(ssd_dt_bias[j], LANES),
                              _row(jnp.repeat(ssd_d[j], SSD_HEAD_DIM)), _row(ssd_norm[j]), ssd_w_out[j].astype(BF16))
        if m == 1:
            lead = jnp.zeros((1, hv), F32)
            alog = jnp.pad(jnp.concatenate([lead, _row(gdn_a_log[j])], axis=1), ((0, 0), (0, LANES - 3 * hv)))
            bias = jnp.pad(jnp.concatenate([lead, _row(gdn_dt_bias[j])], axis=1), ((0, 0), (0, LANES - 3 * hv)))
            return _gdn_mixer(h, b, l, g, _pad_cols(gdn_w_in[j], PROJ_PAD).astype(BF16), gdn_conv_w[j],
                              _row(gdn_conv_b[j]), alog, bias, _row(gdn_norm[j]), gdn_w_out[j].astype(BF16))
        if m == 2:
            cos2, sin2 = _rope_tables(l)
            return _att_mixer(h, b, l, g, att_w_qkv[j].astype(BF16), cos2, sin2,
                              att_sinks[j].astype(F32).reshape(att_steps, 1, -1), att_w_out[j].astype(BF16))
        w1 = jnp.concatenate([rwkv_w1[j, 0], rwkv_w1[j, 1]], axis=1).astype(BF16)
        return _rwkv_mixer(h, b, l, g, rwkv_x_mu[j], rwkv_w_rkv[j, 0].astype(BF16), rwkv_w_rkv[j, 1].astype(BF16),
                           rwkv_w_rkv[j, 2].astype(BF16), w1, rwkv_w2[j].astype(BF16), rwkv_w0[j],
                           _row(rwkv_a0[j]), rwkv_a1[j].astype(BF16), rwkv_a2[j].astype(BF16),
                           rwkv_g1[j].astype(BF16), rwkv_g2[j].astype(BF16), _row(rwkv_k_k[j]), _row(rwkv_k_a[j]),
                           _row(rwkv_r_k[j]), _row(rwkv_lnx_w[j]), _row(rwkv_lnx_b[j]), rwkv_w_out[j].astype(BF16))

    def trunk(x3):
        b, l, d = x3.shape
        x = x3.reshape(b * l, d)
        for i in range(depth):
            x = _ffn(x, _row(ffn1_norm[i]), *ffn1[i], fin_g, False)
            x = mixer(x, b, l, i)
            x = _ffn(x, _row(ffn2_norm[i]), *ffn2[i], fin_g, i == depth - 1)
        return x.reshape(b, l, d)

    return (trunk(x_prompt), trunk(x_sample))
```
